```python
import math
import jax, jax.numpy as jnp
from jax import lax
import numpy as np

D_MODEL = 2048
BATCH = 1
SEQ = 16384
DEPTH = 4

GRID_W = 64
CTX_LEN = 256
HEAD_DIM = 128
N_GROUPS = 4
GROUP_W = D_MODEL // N_GROUPS
N_GROUP_HEADS = GROUP_W // HEAD_DIM
N_KV_HEADS = 2
N_REP = N_GROUP_HEADS // N_KV_HEADS
KV_W = N_KV_HEADS * HEAD_DIM
ATT_W = GROUP_W + 2 * KV_W
CONV_OFF = 0
SGU_OFF = CONV_OFF + 2 * GROUP_W
SWA_OFF = SGU_OFF + 2 * GROUP_W
GLB_OFF = SWA_OFF + ATT_W
IN_W = GLB_OFF + ATT_W
CONV_W = 31
CHUNK = 128
WINDOW = 128
BLOCK_Q = 128
ROPE_THETA = 10000.0
D_FF = 5632
N_EXPERTS = 8
TOP_K = 2
D_FF_EXPERT = 5632
MOE_BLOCK = 512
EPS = 1e-6
NEG = -1e30
SCALE = HEAD_DIM ** -0.5

kernel_name = 'hybrid_parallel_group_flow_backbone'


def rms_norm(x, g):
    xf = x.astype(jnp.float32)
    y = xf * lax.rsqrt(jnp.mean(xf * xf, axis=-1, keepdims=True) + EPS)
    return (y * g.astype(jnp.float32)).astype(x.dtype)


def layer_norm(x, g, b):
    xf = x.astype(jnp.float32)
    xc = xf - jnp.mean(xf, axis=-1, keepdims=True)
    y = xc * lax.rsqrt(jnp.mean(xc * xc, axis=-1, keepdims=True) + EPS)
    return (y * g.astype(jnp.float32) + b.astype(jnp.float32)).astype(x.dtype)


def modulate(h, shift, scale):
    return h * (1 + scale) + shift


def group_rms_norm(y, g):
    shp = y.shape
    yg = y.reshape(shp[:-1] + (N_GROUPS, GROUP_W))
    return rms_norm(yg, g.reshape(N_GROUPS, GROUP_W)).reshape(shp)


def rope_tables(n_rows):
    row = jnp.repeat(jnp.arange(n_rows, dtype=jnp.float32), GRID_W)
    col = jnp.tile(jnp.arange(GRID_W, dtype=jnp.float32), n_rows)
    axis_dim = HEAD_DIM // 2
    inv_freq = ROPE_THETA ** (-jnp.arange(0, axis_dim, 2, dtype=jnp.float32) / axis_dim)
    ang_r = row[:, None] * inv_freq[None, :]
    ang_c = col[:, None] * inv_freq[None, :]
    return (jnp.cos(ang_r), jnp.sin(ang_r), jnp.cos(ang_c), jnp.sin(ang_c))


def rope_1d(x, cos, sin):
    m = cos.shape[-1]
    x1, x2 = x[..., :m], x[..., m:]
    cc, ss = cos[:, None, :], sin[:, None, :]
    return jnp.concatenate([x1 * cc - x2 * ss, x2 * cc + x1 * ss], axis=-1)


def rope_2d(x, tabs):
    cr, sr, cc, sc = tabs
    half = HEAD_DIM // 2
    y = jnp.concatenate([rope_1d(x[..., :half], cr, sr), rope_1d(x[..., half:], cc, sc)], axis=-1)
    return y.astype(x.dtype)


def q_heads(p_q, q_g):
    B, T, _ = p_q.shape
    return rms_norm(p_q.reshape(B, T, N_GROUP_HEADS, HEAD_DIM), q_g)


def kv_heads(p_kv, k_g):
    B, T, _ = p_kv.shape
    k = rms_norm(p_kv[..., :KV_W].reshape(B, T, N_KV_HEADS, HEAD_DIM), k_g)
    v = p_kv[..., KV_W:].reshape(B, T, N_KV_HEADS, HEAD_DIM)
    return k, v


def conformer_conv(p, conv_w, conv_b, ln_g, ln_b, pw, pw_b):
    u = p[..., :GROUP_W] * jax.nn.sigmoid(p[..., GROUP_W:])
    u = lax.conv_general_dilated(u, conv_w[:, None, :], window_strides=(1,),
                                 padding=[(CONV_W // 2, CONV_W // 2)],
                                 dimension_numbers=('NWC', 'WIO', 'NWC'),
                                 feature_group_count=GROUP_W) + conv_b
    u = jax.nn.silu(layer_norm(u, ln_g, ln_b))
    return u @ pw + pw_b


def chunk_sgu(p, ln_g, ln_b, w_s, b_s):
    z = jax.nn.gelu(p)
    u, v = z[..., :GROUP_W], layer_norm(z[..., GROUP_W:], ln_g, ln_b)
    B, T, _ = v.shape
    vc = v.reshape(B, T // CHUNK, CHUNK, N_GROUP_HEADS, HEAD_DIM)
    s = jnp.einsum('gpq,bnqgc->bnpgc', w_s, vc) + b_s.T[None, None, :, :, None]
    return u * s.reshape(B, T, GROUP_W)


def ctx_attn(q, k, v, sink):
    B, T = q.shape[:2]
    qg = q.reshape(B, T, N_KV_HEADS, N_REP, HEAD_DIM)
    s = jnp.einsum('bqkgd,bskd->bkgqs', qg, k).astype(jnp.float32) * SCALE
    if sink is not None:
        sb = jnp.broadcast_to(sink.reshape(N_KV_HEADS, N_REP)[None, :, :, None, None].astype(jnp.float32),
                              s.shape[:-1] + (1,))
        s = jnp.concatenate([sb, s], axis=-1)
    pr = jax.nn.softmax(s, axis=-1)[..., -k.shape[1]:].astype(v.dtype)
    o = jnp.einsum('bkgqs,bskd->bqkgd', pr, v)
    return o.reshape(B, T, GROUP_W)


def window_attn(q, k, v, k_ctx, v_ctx, sink):
    B, L = q.shape[:2]
    nb = L // BLOCK_Q
    qb = q.reshape(B, nb, BLOCK_Q, N_KV_HEADS, N_REP, HEAD_DIM)
    pad = ((0, 0), (BLOCK_Q, BLOCK_Q), (0, 0), (0, 0))
    kp = jnp.pad(k, pad).reshape(B, nb + 2, BLOCK_Q, N_KV_HEADS, HEAD_DIM)
    vp = jnp.pad(v, pad).reshape(B, nb + 2, BLOCK_Q, N_KV_HEADS, HEAD_DIM)
    kw = jnp.concatenate([kp[:, :-2], kp[:, 1:-1], kp[:, 2:]], axis=2)
    vw = jnp.concatenate([vp[:, :-2], vp[:, 1:-1], vp[:, 2:]], axis=2)
    s_win = jnp.einsum('bnqkgd,bnskd->bnkgqs', qb, kw).astype(jnp.float32) * SCALE
    qpos = jnp.arange(nb)[:, None] * BLOCK_Q + jnp.arange(BLOCK_Q)[None, :]
    kpos = (jnp.arange(nb)[:, None] - 1) * BLOCK_Q + jnp.arange(3 * BLOCK_Q)[None, :]
    ok = ((jnp.abs(qpos[:, :, None] - kpos[:, None, :]) <= WINDOW)
          & (kpos[:, None, :] >= 0) & (kpos[:, None, :] < L))
    s_win = jnp.where(ok[None, :, None, None], s_win, NEG)
    s_ctx = jnp.einsum('bnqkgd,bckd->bnkgqc', qb, k_ctx).astype(jnp.float32) * SCALE
    sb = jnp.broadcast_to(sink.reshape(N_KV_HEADS, N_REP)[None, None, :, :, None, None].astype(jnp.float32),
                          s_ctx.shape[:-1] + (1,))
    pr = jax.nn.softmax(jnp.concatenate([sb, s_ctx, s_win], axis=-1), axis=-1)
    n_c = k_ctx.shape[1]
    p_ctx = pr[..., 1:1 + n_c].astype(v.dtype)
    p_win = pr[..., 1 + n_c:].astype(v.dtype)
    o = (jnp.einsum('bnkgqc,bckd->bnqkgd', p_ctx, v_ctx)
         + jnp.einsum('bnkgqs,bnskd->bnqkgd', p_win, vw))
    return o.reshape(B, L, GROUP_W)


def global_attn(q, k, v, k_ctx, v_ctx):
    B, L = q.shape[:2]
    nb = L // BLOCK_Q
    k_all = jnp.concatenate([k_ctx, k], axis=1)
    v_all = jnp.concatenate([v_ctx, v], axis=1)
    qb = q.reshape(B, nb, BLOCK_Q, N_KV_HEADS, N_REP, HEAD_DIM).transpose(1, 0, 2, 3, 4, 5)

    def one_block(q_blk):
        s = jnp.einsum('bqkgd,bskd->bkgqs', q_blk, k_all).astype(jnp.float32) * SCALE
        pr = jax.nn.softmax(s, axis=-1).astype(v_all.dtype)
        return jnp.einsum('bkgqs,bskd->bqkgd', pr, v_all)

    o = lax.map(one_block, qb)
    return o.transpose(1, 0, 2, 3, 4, 5).reshape(B, L, GROUP_W)


def swiglu(h, w1, w3, w2):
    return (jax.nn.silu(h @ w1) * (h @ w3)) @ w2


def moe_swiglu(h, w_r, b_r, w1, w3, w2):
    n_tok, d = h.shape
    logits = (h @ w_r + b_r).astype(jnp.float32)
    top_v, top_e = lax.top_k(logits, TOP_K)
    gates = jax.nn.softmax(top_v, axis=-1).astype(h.dtype)
    n_slot = n_tok * TOP_K
    slot_e = top_e.reshape(n_slot)
    slot_tok = jnp.arange(n_slot, dtype=jnp.int32) // TOP_K
    slot_g = gates.reshape(n_slot)
    order = jnp.argsort(slot_e)
    sorted_e = slot_e[order]
    counts = jnp.bincount(slot_e, length=N_EXPERTS).astype(jnp.int32)
    padded = (counts + MOE_BLOCK - 1) // MOE_BLOCK * MOE_BLOCK
    pad_end = jnp.cumsum(padded)
    pad_start = pad_end - padded
    start = jnp.cumsum(counts) - counts
    dest = pad_start[sorted_e] + jnp.arange(n_slot, dtype=jnp.int32) - start[sorted_e]
    n_rows = -(-(n_slot + N_EXPERTS * (MOE_BLOCK - 1)) // MOE_BLOCK) * MOE_BLOCK
    n_blocks = n_rows // MOE_BLOCK
    row_tok = jnp.full((n_rows,), n_tok, jnp.int32).at[dest].set(slot_tok[order])
    row_g = jnp.zeros((n_rows,), h.dtype).at[dest].set(slot_g[order])
    blk_e = jnp.minimum(jnp.searchsorted(pad_end, jnp.arange(n_blocks, dtype=jnp.int32) * MOE_BLOCK,
                                         side='right'), N_EXPERTS - 1)
    h_pad = jnp.concatenate([h, jnp.zeros((1, d), h.dtype)], axis=0)
    xb = h_pad[row_tok].reshape(n_blocks, MOE_BLOCK, d)

    def expert_block(args):
        x_blk, e = args
        return (jax.nn.silu(x_blk @ w1[e]) * (x_blk @ w3[e])) @ w2[e]

    yb = lax.map(expert_block, (xb, blk_e))
    y = yb.reshape(n_rows, d) * row_g[:, None]
    return jnp.zeros((n_tok + 1, d), h.dtype).at[row_tok].add(y)[:n_tok]


def setup_inputs(seed: int = 0) -> dict:
    key = jax.random.key(seed)
    ks = iter(jax.random.split(key, 48))
    f32 = jnp.float32

    def nrm(shape, scale):
        return jax.random.normal(next(ks), shape, f32) * scale

    D, G, H = D_MODEL, GROUP_W, N_GROUP_HEADS
    n_dense = (DEPTH + 1) // 2
    n_moe = DEPTH // 2
    return {
        'x': nrm((BATCH, SEQ, D), 1.0),
        'c': nrm((BATCH, D), 1.0),
        'ctx': nrm((BATCH, CTX_LEN, D), 1.0),
        'c_ctx': nrm((D,), 1.0),
        'w_ada': nrm((DEPTH, D, 6 * D), 0.5 * D ** -0.5),
        'b_ada': nrm((DEPTH, 6 * D), 0.01),
        'g_norm1': 1.0 + nrm((DEPTH, D), 0.02),
        'w_in': nrm((DEPTH, D, IN_W), D ** -0.5),
        'conv_w': nrm((DEPTH, CONV_W, G), CONV_W ** -0.5),
        'conv_b': nrm((DEPTH, G), 0.01),
        'conv_ln_g': 1.0 + nrm((DEPTH, G), 0.02),
        'conv_ln_b': nrm((DEPTH, G), 0.01),
        'conv_pw': nrm((DEPTH, G, G), G ** -0.5),
        'conv_pw_b': nrm((DEPTH, G), 0.01),
        'sgu_ln_g': 1.0 + nrm((DEPTH, G), 0.02),
        'sgu_ln_b': nrm((DEPTH, G), 0.01),
        'sgu_w': nrm((DEPTH, H, CHUNK, CHUNK), CHUNK ** -0.5),
        'sgu_b': 1.0 + nrm((DEPTH, H, CHUNK), 0.01),
        'swa_q_g': 1.0 + nrm((DEPTH, HEAD_DIM), 0.02),
        'swa_k_g': 1.0 + nrm((DEPTH, HEAD_DIM), 0.02),
        'swa_sink': nrm((DEPTH, H), 0.5),
        'glb_q_g': 1.0 + nrm((DEPTH, HEAD_DIM), 0.02),
        'glb_k_g': 1.0 + nrm((DEPTH, HEAD_DIM), 0.02),
        'g_branch': 1.0 + nrm((DEPTH, D), 0.02),
        'w_out': nrm((DEPTH, D, D), D ** -0.5),
        'g_norm2': 1.0 + nrm((DEPTH, D), 0.02),
        'ffn_w1': nrm((n_dense, D, D_FF), D ** -0.5),
        'ffn_w3': nrm((n_dense, D, D_FF), D ** -0.5),
        'ffn_w2': nrm((n_dense, D_FF, D), D_FF ** -0.5),
        'router_w': nrm((n_moe, D, N_EXPERTS), D ** -0.5),
        'router_b': nrm((n_moe, N_EXPERTS), 0.01),
        'exp_w1': nrm((n_moe, N_EXPERTS, D, D_FF_EXPERT), D ** -0.5),
        'exp_w3': nrm((n_moe, N_EXPERTS, D, D_FF_EXPERT), D ** -0.5),
        'exp_w2': nrm((n_moe, N_EXPERTS, D_FF_EXPERT, D), D_FF_EXPERT ** -0.5),
    }


def reference(x, c, ctx, c_ctx, w_ada, b_ada, g_norm1, w_in, conv_w, conv_b, conv_ln_g, conv_ln_b,
              conv_pw, conv_pw_b, sgu_ln_g, sgu_ln_b, sgu_w, sgu_b, swa_q_g, swa_k_g, swa_sink,
              glb_q_g, glb_k_g, g_branch, w_out, g_norm2, ffn_w1, ffn_w3, ffn_w2,
              router_w, router_b, exp_w1, exp_w3, exp_w2):
    B, L, D = x.shape
    n_c = ctx.shape[1]
    n_rows = L // GRID_W
    tabs = rope_tables(n_rows)
    x_lat, x_ctx = x, ctx
    for l in range(DEPTH):
        last = l == DEPTH - 1
        m_lat = jax.nn.silu(c) @ w_ada[l] + b_ada[l]
        m_ctx = jax.nn.silu(c_ctx) @ w_ada[l] + b_ada[l]
        sh1, sc1, gt1, sh2, sc2, gt2 = jnp.split(m_lat[:, None, :], 6, axis=-1)
        csh1, csc1, cgt1, csh2, csc2, cgt2 = jnp.split(m_ctx, 6)
        w = w_in[l]

        h_lat = modulate(rms_norm(x_lat, g_norm1[l]), sh1, sc1)
        h_ctx = modulate(rms_norm(x_ctx, g_norm1[l]), csh1, csc1)
        p_lat = h_lat @ w
        if last:
            w_kv = jnp.concatenate([w[:, SWA_OFF + GROUP_W:SWA_OFF + ATT_W],
                                    w[:, GLB_OFF + GROUP_W:GLB_OFF + ATT_W]], axis=1)
            p_kv = h_ctx @ w_kv
            swa_kv_c, glb_kv_c = p_kv[..., :2 * KV_W], p_kv[..., 2 * KV_W:]
        else:
            p_ctx = h_ctx @ w
            swa_kv_c = p_ctx[..., SWA_OFF + GROUP_W:SWA_OFF + ATT_W]
            glb_kv_c = p_ctx[..., GLB_OFF + GROUP_W:GLB_OFF + ATT_W]
        k_sc, v_sc = kv_heads(swa_kv_c, swa_k_g[l])
        k_gc, v_gc = kv_heads(glb_kv_c, glb_k_g[l])

        conv_args = (conv_w[l], conv_b[l], conv_ln_g[l], conv_ln_b[l], conv_pw[l], conv_pw_b[l])
        sgu_args = (sgu_ln_g[l], sgu_ln_b[l], sgu_w[l], sgu_b[l])

        y_conv = conformer_conv(p_lat[..., CONV_OFF:SGU_OFF], *conv_args)
        y_sgu = chunk_sgu(p_lat[..., SGU_OFF:SWA_OFF], *sgu_args)
        p_swa = p_lat[..., SWA_OFF:GLB_OFF]
        q_s = rope_2d(q_heads(p_swa[..., :GROUP_W], swa_q_g[l]), tabs)
        k_s, v_s = kv_heads(p_swa[..., GROUP_W:], swa_k_g[l])
        y_swa = window_attn(q_s, rope_2d(k_s, tabs), v_s, k_sc, v_sc, swa_sink[l])
        p_glb = p_lat[..., GLB_OFF:IN_W]
        q_g = rope_2d(q_heads(p_glb[..., :GROUP_W], glb_q_g[l]), tabs)
        k_g, v_g = kv_heads(p_glb[..., GROUP_W:], glb_k_g[l])
        y_glb = global_attn(q_g, rope_2d(k_g, tabs), v_g, k_gc, v_gc)
        y = group_rms_norm(jnp.concatenate([y_conv, y_sgu, y_swa, y_glb], axis=-1), g_branch[l])
        x_lat = x_lat + gt1 * (y @ w_out[l])

        if not last:
            yc_conv = conformer_conv(p_ctx[..., CONV_OFF:SGU_OFF], *conv_args)
            yc_sgu = chunk_sgu(p_ctx[..., SGU_OFF:SWA_OFF], *sgu_args)
            yc_swa = ctx_attn(q_heads(p_ctx[..., SWA_OFF:SWA_OFF + GROUP_W], swa_q_g[l]),
                              k_sc, v_sc, swa_sink[l])
            yc_glb = ctx_attn(q_heads(p_ctx[..., GLB_OFF:GLB_OFF + GROUP_W], glb_q_g[l]),
                              k_gc, v_gc, None)
            yc = group_rms_norm(jnp.concatenate([yc_conv, yc_sgu, yc_swa, yc_glb], axis=-1), g_branch[l])
            x_ctx = x_ctx + cgt1 * (yc @ w_out[l])

        h2 = modulate(rms_norm(x_lat, g_norm2[l]), sh2, sc2)
        if not last:
            h2c = modulate(rms_norm(x_ctx, g_norm2[l]), csh2, csc2)
            h2 = jnp.concatenate([h2c, h2], axis=1)
        flat = h2.reshape(-1, D)
        if l % 2 == 0:
            f = swiglu(flat, ffn_w1[l // 2], ffn_w3[l // 2], ffn_w2[l // 2])
        else:
            f = moe_swiglu(flat, router_w[l // 2], router_b[l // 2],
                           exp_w1[l // 2], exp_w3[l // 2], exp_w2[l // 2])
        f = f.reshape(B, -1, D)
        if last:
            x_lat = x_lat + gt2 * f
        else:
            x_ctx = x_ctx + cgt2 * f[:, :n_c]
            x_lat = x_lat + gt2 * f[:, n_c:]
    return x_lat
```

```python
import functools
import math

import jax
import jax.numpy as jnp
from jax import lax
from jax.experimental import pallas as pl
from jax.experimental.pallas import tpu as pltpu

F32 = jnp.float32
BF16 = jnp.bfloat16
I32 = jnp.int32

HEAD_DIM = 128
N_GROUPS = 4
N_KV_HEADS = 2
GRID_W = 64
WINDOW = 128
ROPE_THETA = 10000.0
TOP_K = 2
MOE_BLOCK = 512
EPS = 1e-6
NEG = -1e30
SCALE = HEAD_DIM ** -0.5

V7X_VMEM_BYTES = 64 * 1024 * 1024
V7X_LANES = 128
V7X_SUBLANES = 8
VMEM_BUDGET = V7X_VMEM_BYTES - 8 * 1024 * 1024

ROW_TILE = 256
CONV_HALO = 16
NORM_CHUNK = 128


def _pick(n, cands):
    for c in cands:
        if n % c == 0:
            return c
    raise ValueError(f"no tile in {cands} divides {n}")


def _params(sem, vmem=VMEM_BUDGET):
    return pltpu.CompilerParams(dimension_semantics=sem, vmem_limit_bytes=vmem)


def _sigmoid(x):
    return 1.0 / (1.0 + jnp.exp(-x))


def _silu(x):
    return x * _sigmoid(x)


def _gelu_tanh(x):
    c = math.sqrt(2.0 / math.pi)
    return 0.5 * x * (1.0 + jnp.tanh(c * (x + 0.044715 * (x * x * x))))


def _rms(x):
    return x * lax.rsqrt(jnp.mean(x * x, axis=-1, keepdims=True) + EPS)


def _layer_norm(x, g, b):
    xc = x - jnp.mean(x, axis=-1, keepdims=True)
    return xc * lax.rsqrt(jnp.mean(xc * xc, axis=-1, keepdims=True) + EPS) * g + b


def _row_select(vec2_ref, row0, tm, n_lat):
    rows = row0 + lax.broadcasted_iota(I32, (tm, 1), 0)
    return jnp.where(rows >= n_lat, vec2_ref[1:2, :], vec2_ref[0:1, :])


def _norm_mod(x, g, shift, scale):
    return _rms(x) * g * (1.0 + scale) + shift


def _norm_mod_rows(x_ref, g_ref, sh_ref, sc_ref, out_ref, row0, n_lat):
    tm = x_ref.shape[0]

    def body(c, carry):
        r0 = pl.multiple_of(c * NORM_CHUNK, NORM_CHUNK)
        rows = pl.ds(r0, NORM_CHUNK)
        sh = _row_select(sh_ref, row0 + r0, NORM_CHUNK, n_lat)
        sc = _row_select(sc_ref, row0 + r0, NORM_CHUNK, n_lat)
        out_ref[rows, :] = _norm_mod(x_ref[rows, :], g_ref[...], sh, sc).astype(out_ref.dtype)
        return carry

    lax.fori_loop(0, tm // NORM_CHUNK, body, 0)


def _ada_kernel(s_ref, w_ref, b_ref, o_ref):
    d, tn = w_ref.shape
    kc = 64

    def body(c, acc):
        a0, a1 = acc
        k0 = pl.multiple_of(c * kc, kc)
        w = w_ref[pl.ds(k0, kc), :]
        s = _silu(s_ref[pl.ds(k0, kc), :])
        a0 = a0 + (w * s[:, 0:1]).reshape(kc // V7X_SUBLANES, V7X_SUBLANES, tn).sum(axis=0)
        a1 = a1 + (w * s[:, 1:2]).reshape(kc // V7X_SUBLANES, V7X_SUBLANES, tn).sum(axis=0)
        return a0, a1

    z = jnp.zeros((V7X_SUBLANES, tn), F32)
    a0, a1 = lax.fori_loop(0, d // kc, body, (z, z))
    o_ref[0:1, :] = a0.sum(axis=0, keepdims=True) + b_ref[...]
    o_ref[1:2, :] = a1.sum(axis=0, keepdims=True) + b_ref[...]


def _ada(cond, w_ada, b_ada):
    depth, d, n = w_ada.shape
    tn = _pick(n, (1024, 512, 256, 128))
    return pl.pallas_call(
        _ada_kernel,
        out_shape=jax.ShapeDtypeStruct((depth, 2, n), F32),
        grid=(depth, n // tn),
        in_specs=[
            pl.BlockSpec((d, 2), lambda l, j: (0, 0)),
            pl.BlockSpec((None, d, tn), lambda l, j: (l, 0, j)),
            pl.BlockSpec((None, 1, tn), lambda l, j: (l, 0, j)),
        ],
        out_specs=pl.BlockSpec((None, 2, tn), lambda l, j: (l, 0, j)),
        compiler_params=_params(("parallel", "parallel")),
        name="ada",
    )(cond, w_ada, b_ada.reshape(depth, 1, n))


def _inproj_kernel(x_ref, g_ref, sh_ref, sc_ref, w_ref, o_ref, h_s, *, n_lat):
    i = pl.program_id(0)
    tm = x_ref.shape[0]

    @pl.when(pl.program_id(1) == 0)
    def _():
        _norm_mod_rows(x_ref, g_ref, sh_ref, sc_ref, h_s, i * tm, n_lat)

    o_ref[...] = jnp.dot(h_s[...], w_ref[...], preferred_element_type=F32)


def _inproj(x, g, sh, sc, w, n_lat):
    r, d = x.shape
    n = w.shape[1]
    tm = _pick(r, (640, 256))
    tn = _pick(n, (1024, 512, 256))
    return pl.pallas_call(
        functools.partial(_inproj_kernel, n_lat=n_lat),
        out_shape=jax.ShapeDtypeStruct((r, n), F32),
        grid=(r // tm, n // tn),
        in_specs=[
            pl.BlockSpec((tm, d), lambda i, j: (i, 0)),
            pl.BlockSpec((1, d), lambda i, j: (0, 0)),
            pl.BlockSpec((2, d), lambda i, j: (0, 0)),
            pl.BlockSpec((2, d), lambda i, j: (0, 0)),
            pl.BlockSpec((d, tn), lambda i, j: (0, j)),
        ],
        out_specs=pl.BlockSpec((tm, tn), lambda i, j: (i, j)),
        scratch_shapes=[pltpu.VMEM((tm, d), BF16)],
        compiler_params=_params(("parallel", "arbitrary")),
        name="inproj",
    )(x, g, sh, sc, w)


def _conv_kernel(pm_ref, pp_ref, pn_ref, cw_ref, cb_ref, lg_ref, lb_ref, pw_ref, pb_ref, o_ref, ext_s,
                 *, n_lat_tiles, n_tiles):
    i = pl.program_id(0)
    tm, gw = o_ref.shape
    kw = cw_ref.shape[0]

    def glu(p):
        return p[:, :gw] * _sigmoid(p[:, gw:])

    prev_ok = jnp.logical_and(i != 0, i != n_lat_tiles)
    next_ok = jnp.logical_and(i != n_lat_tiles - 1, i != n_tiles - 1)
    ext_s[0:CONV_HALO, :] = jnp.where(prev_ok, glu(pp_ref[...]), 0.0)
    ext_s[CONV_HALO:CONV_HALO + tm, :] = glu(pm_ref[...])
    ext_s[CONV_HALO + tm:, :] = jnp.where(next_ok, glu(pn_ref[...]), 0.0)

    acc = jnp.zeros((tm, gw), F32) + cb_ref[...]
    for k in range(kw):
        off = CONV_HALO - kw // 2 + k
        acc = acc + cw_ref[k:k + 1, :] * ext_s[off:off + tm, :]
    y = _silu(_layer_norm(acc, lg_ref[...], lb_ref[...]))
    o_ref[...] = jnp.dot(y.astype(BF16), pw_ref[...], preferred_element_type=F32) + pb_ref[...]


def _conv_group(p, cw, cb, lg, lb, pw, pb, n_lat):
    r = p.shape[0]
    kw, gw = cw.shape
    tm = ROW_TILE
    n_tiles = r // tm
    hb = tm // CONV_HALO
    last_hb = r // CONV_HALO - 1
    return pl.pallas_call(
        functools.partial(_conv_kernel, n_lat_tiles=n_lat // tm, n_tiles=n_tiles),
        out_shape=jax.ShapeDtypeStruct((r, gw), F32),
        grid=(n_tiles,),
        in_specs=[
            pl.BlockSpec((tm, 2 * gw), lambda i: (i, 0)),
            pl.BlockSpec((CONV_HALO, 2 * gw), lambda i: (jnp.maximum(i * hb - 1, 0), 0)),
            pl.BlockSpec((CONV_HALO, 2 * gw), lambda i: (jnp.minimum((i + 1) * hb, last_hb), 0)),
            pl.BlockSpec((kw, gw), lambda i: (0, 0)),
            pl.BlockSpec((1, gw), lambda i: (0, 0)),
            pl.BlockSpec((1, gw), lambda i: (0, 0)),
            pl.BlockSpec((1, gw), lambda i: (0, 0)),
            pl.BlockSpec((gw, gw), lambda i: (0, 0)),
            pl.BlockSpec((1, gw), lambda i: (0, 0)),
        ],
        out_specs=pl.BlockSpec((tm, gw), lambda i: (i, 0)),
        scratch_shapes=[pltpu.VMEM((tm + 2 * CONV_HALO, gw), F32)],
        compiler_params=_params(("parallel",)),
        name="conv_group",
    )(p, p, p, cw, cb, lg, lb, pw, pb)


def _sgu_kernel(p_ref, lg_ref, lb_ref, ws_ref, bs_ref, o_ref):
    tm, gw = o_ref.shape
    n_h, ch, _ = ws_ref.shape
    hd = gw // n_h
    z = _gelu_tanh(p_ref[...])
    u = z[:, :gw]
    v = _layer_norm(z[:, gw:], lg_ref[...], lb_ref[...]).astype(BF16)
    for c in range(tm // ch):
        rows = slice(c * ch, (c + 1) * ch)
        parts = []
        for h in range(n_h):
            s = jnp.dot(ws_ref[h], v[rows, h * hd:(h + 1) * hd], preferred_element_type=F32)
            parts.append(s + bs_ref[:, h:h + 1])
        o_ref[rows, :] = u[rows, :] * jnp.concatenate(parts, axis=1)


def _sgu_group(p, lg, lb, ws, bs_t):
    r = p.shape[0]
    gw = lg.shape[1]
    n_h, ch, _ = ws.shape
    tm = ROW_TILE
    return pl.pallas_call(
        _sgu_kernel,
        out_shape=jax.ShapeDtypeStruct((r, gw), F32),
        grid=(r // tm,),
        in_specs=[
            pl.BlockSpec((tm, 2 * gw), lambda i: (i, 1)),
            pl.BlockSpec((1, gw), lambda i: (0, 0)),
            pl.BlockSpec((1, gw), lambda i: (0, 0)),
            pl.BlockSpec((n_h, ch, ch), lambda i: (0, 0, 0)),
            pl.BlockSpec((ch, n_h), lambda i: (0, 0)),
        ],
        out_specs=pl.BlockSpec((tm, gw), lambda i: (i, 0)),
        compiler_params=_params(("parallel",)),
        name="sgu_group",
    )(p, lg, lb, ws, bs_t)


def _prep_kernel(p_ref, qg_ref, kg_ref, cos_ref, sin_ref, q_ref, k_ref, v_ref):
    tm = p_ref.shape[0]
    qw, kvw = q_ref.shape[1], k_ref.shape[1]
    cos = cos_ref[...]
    sin = sin_ref[...]
    lane = lax.broadcasted_iota(I32, (tm, HEAD_DIM), 1)
    low = (lane & (HEAD_DIM // 4)) == 0

    def head(x, g, mult):
        y = _rms(x) * g
        partner = jnp.where(low, pltpu.roll(y, HEAD_DIM - HEAD_DIM // 4, 1), pltpu.roll(y, HEAD_DIM // 4, 1))
        return ((y * cos + partner * sin) * mult).astype(BF16)

    for h in range(qw // HEAD_DIM):
        cols = slice(h * HEAD_DIM, (h + 1) * HEAD_DIM)
        q_ref[:, cols] = head(p_ref[:, cols], qg_ref[...], SCALE)
    for h in range(kvw // HEAD_DIM):
        cols = slice(h * HEAD_DIM, (h + 1) * HEAD_DIM)
        k_ref[:, cols] = head(p_ref[:, qw + h * HEAD_DIM:qw + (h + 1) * HEAD_DIM], kg_ref[...], 1.0)
    v_ref[...] = p_ref[:, qw + kvw:].astype(BF16)


def _prep(p, qg, kg, cos_t, sin_t, first_block):
    r = p.shape[0]
    n_att = qg.shape[0]
    qw = N_GROUPS * HEAD_DIM
    kvw = N_KV_HEADS * HEAD_DIM
    tm = ROW_TILE
    return pl.pallas_call(
        _prep_kernel,
        out_shape=(jax.ShapeDtypeStruct((n_att, r, qw), BF16),
                   jax.ShapeDtypeStruct((n_att, r, kvw), BF16),
                   jax.ShapeDtypeStruct((n_att, r, kvw), BF16)),
        grid=(r // tm, n_att),
        in_specs=[
            pl.BlockSpec((tm, qw + 2 * kvw), lambda i, a: (i, first_block + a)),
            pl.BlockSpec((None, 1, HEAD_DIM), lambda i, a: (a, 0, 0)),
            pl.BlockSpec((None, 1, HEAD_DIM), lambda i, a: (a, 0, 0)),
            pl.BlockSpec((tm, HEAD_DIM), lambda i, a: (i, 0)),
            pl.BlockSpec((tm, HEAD_DIM), lambda i, a: (i, 0)),
        ],
        out_specs=(pl.BlockSpec((None, tm, qw), lambda i, a: (a, i, 0)),
                   pl.BlockSpec((None, tm, kvw), lambda i, a: (a, i, 0)),
                   pl.BlockSpec((None, tm, kvw), lambda i, a: (a, i, 0))),
        compiler_params=_params(("parallel", "parallel")),
        name="qkv_prep",
    )(p, qg, kg, cos_t, sin_t)


def _nt_dot(a, b):
    return lax.dot_general(a, b, (((1,), (1,)), ((), ())), preferred_element_type=F32)


def _swa_kernel(sink_ref, q_ref, km_ref, kp_ref, kn_ref, vm_ref, vp_ref, vn_ref, kc_ref, vc_ref, o_ref,
                *, n_lat):
    i = pl.program_id(0)
    tq = q_ref.shape[0]
    hb = kp_ref.shape[0]
    n_rep = (q_ref.shape[1] // HEAD_DIM) // N_KV_HEADS
    nk = tq + 2 * hb
    qi = lax.broadcasted_iota(I32, (tq, nk), 0)
    kj = lax.broadcasted_iota(I32, (tq, nk), 1) - hb
    kglob = i * tq + kj
    ok1 = (jnp.abs(kj - qi) <= WINDOW) & (kglob >= 0) & (kglob < n_lat) & (i * tq < n_lat)
    ok = jnp.concatenate([ok1.astype(F32)] * n_rep, axis=0) > 0.5
    rep_of_row = jnp.concatenate([jnp.full((tq, 1), g, I32) for g in range(n_rep)], axis=0)
    for j in range(N_KV_HEADS):
        kv = slice(j * HEAD_DIM, (j + 1) * HEAD_DIM)
        q2 = jnp.concatenate(
            [q_ref[:, (j * n_rep + g) * HEAD_DIM:(j * n_rep + g + 1) * HEAD_DIM] for g in range(n_rep)], axis=0)
        kw = jnp.concatenate([kp_ref[:, kv], km_ref[:, kv], kn_ref[:, kv]], axis=0)
        vw = jnp.concatenate([vp_ref[:, kv], vm_ref[:, kv], vn_ref[:, kv]], axis=0)
        s_w = jnp.where(ok, _nt_dot(q2, kw), NEG)
        s_c = _nt_dot(q2, kc_ref[:, kv])
        sk = jnp.zeros((n_rep * tq, 1), F32)
        for g in range(n_rep):
            sk = jnp.where(rep_of_row == g, sink_ref[j * n_rep + g], sk)
        m = jnp.maximum(jnp.maximum(s_w.max(axis=1, keepdims=True), s_c.max(axis=1, keepdims=True)), sk)
        p_w = jnp.exp(s_w - m)
        p_c = jnp.exp(s_c - m)
        den = jnp.exp(sk - m) + p_w.sum(axis=1, keepdims=True) + p_c.sum(axis=1, keepdims=True)
        o = (jnp.dot(p_w.astype(BF16), vw, preferred_element_type=F32)
             + jnp.dot(p_c.astype(BF16), vc_ref[:, kv], preferred_element_type=F32)) / den
        for g in range(n_rep):
            h = j * n_rep + g
            o_ref[:, h * HEAD_DIM:(h + 1) * HEAD_DIM] = o[g * tq:(g + 1) * tq, :]


def _swa(q, k, v, sink, n_lat, grp):
    _, r, qw = q.shape
    kvw = k.shape[2]
    n_ctx = r - n_lat
    tq = ROW_TILE
    hb = WINDOW
    per = tq // hb
    last_hb = r // hb - 1
    main = lambda i: (grp, i, 0)
    prev = lambda i: (grp, jnp.maximum(i * per - 1, 0), 0)
    nxt = lambda i: (grp, jnp.minimum((i + 1) * per, last_hb), 0)
    ctx = lambda i: (grp, n_lat // n_ctx, 0)
    return pl.pallas_call(
        functools.partial(_swa_kernel, n_lat=n_lat),
        out_shape=jax.ShapeDtypeStruct((r, qw), F32),
        grid=(r // tq,),
        in_specs=[
            pl.BlockSpec(memory_space=pltpu.SMEM),
            pl.BlockSpec((None, tq, qw), main),
            pl.BlockSpec((None, tq, kvw), main),
            pl.BlockSpec((None, hb, kvw), prev),
            pl.BlockSpec((None, hb, kvw), nxt),
            pl.BlockSpec((None, tq, kvw), main),
            pl.BlockSpec((None, hb, kvw), prev),
            pl.BlockSpec((None, hb, kvw), nxt),
            pl.BlockSpec((None, n_ctx, kvw), ctx),
            pl.BlockSpec((None, n_ctx, kvw), ctx),
        ],
        out_specs=pl.BlockSpec((tq, qw), lambda i: (i, 0)),
        compiler_params=_params(("parallel",)),
        name="window_attn",
    )(sink, q, k, k, k, v, v, v, k, v)


def _glb_kernel(q_ref, k_ref, v_ref, o_ref, *, n_lat, tk):
    i = pl.program_id(1)
    tq = q_ref.shape[0]
    n_rep = q_ref.shape[1] // HEAD_DIM
    n_ctx = k_ref.shape[0] - n_lat
    q2 = jnp.concatenate([q_ref[:, g * HEAD_DIM:(g + 1) * HEAD_DIM] for g in range(n_rep)], axis=0)

    def step(k0, size, carry):
        m, l, acc = carry
        kc = k_ref[pl.ds(k0, size), :]
        vc = v_ref[pl.ds(k0, size), :]
        s = _nt_dot(q2, kc)
        m_new = jnp.maximum(m, s.max(axis=1, keepdims=True))
        alpha = jnp.exp(m - m_new)
        p = jnp.exp(s - m_new)
        l = alpha * l + p.sum(axis=1, keepdims=True)
        acc = alpha * acc + jnp.dot(p.astype(BF16), vc, preferred_element_type=F32)
        return m_new, l, acc

    init = (jnp.full((n_rep * tq, 1), NEG, F32), jnp.zeros((n_rep * tq, 1), F32),
            jnp.zeros((n_rep * tq, HEAD_DIM), F32))
    n_chunks = jnp.where(i * tq < n_lat, n_lat // tk, 0)
    carry = lax.fori_loop(0, n_chunks, lambda c, cr: step(pl.multiple_of(c * tk, tk), tk, cr), init)
    _, l, acc = step(n_lat, n_ctx, carry)
    o = acc / l
    for g in range(n_rep):
        o_ref[:, g * HEAD_DIM:(g + 1) * HEAD_DIM] = o[g * tq:(g + 1) * tq, :]


def _glb(q, k, v, n_lat, grp):
    _, r, qw = q.shape
    tq = ROW_TILE
    gq = qw // N_KV_HEADS
    tk = _pick(n_lat, (1024, 512, 256))
    return pl.pallas_call(
        functools.partial(_glb_kernel, n_lat=n_lat, tk=tk),
        out_shape=jax.ShapeDtypeStruct((r, qw), F32),
        grid=(N_KV_HEADS, r // tq),
        in_specs=[
            pl.BlockSpec((None, tq, gq), lambda j, i: (grp, i, j)),
            pl.BlockSpec((None, r, HEAD_DIM), lambda j, i: (grp, 0, j)),
            pl.BlockSpec((None, r, HEAD_DIM), lambda j, i: (grp, 0, j)),
        ],
        out_specs=pl.BlockSpec((tq, gq), lambda j, i: (i, j)),
        compiler_params=_params(("parallel", "parallel")),
        name="global_attn",
    )(q, k, v)


def _outproj_kernel(y0_ref, y1_ref, y2_ref, y3_ref, gb_ref, w_ref, x_ref, gt_ref, o_ref, yn_s, *, n_lat):
    i = pl.program_id(0)
    tm = x_ref.shape[0]

    @pl.when(pl.program_id(1) == 0)
    def _():
        for g, y_ref in enumerate((y0_ref, y1_ref, y2_ref, y3_ref)):
            gw = y_ref.shape[1]
            cols = slice(g * gw, (g + 1) * gw)
            yn_s[:, cols] = (_rms(y_ref[...]) * gb_ref[:, cols]).astype(BF16)

    gate = _row_select(gt_ref, i * tm, tm, n_lat)
    o_ref[...] = x_ref[...] + gate * jnp.dot(yn_s[...], w_ref[...], preferred_element_type=F32)


def _outproj(ys, gb, w, x, gt, n_lat):
    r, d = x.shape
    gw = ys[0].shape[1]
    tm = _pick(r, (640, 256))
    tn = _pick(d, (1024, 512, 256))
    ysp = pl.BlockSpec((tm, gw), lambda i, j: (i, 0))
    return pl.pallas_call(
        functools.partial(_outproj_kernel, n_lat=n_lat),
        out_shape=jax.ShapeDtypeStruct((r, d), F32),
        grid=(r // tm, d // tn),
        in_specs=[ysp, ysp, ysp, ysp,
                  pl.BlockSpec((1, d), lambda i, j: (0, 0)),
                  pl.BlockSpec((d, tn), lambda i, j: (0, j)),
                  pl.BlockSpec((tm, tn), lambda i, j: (i, j)),
                  pl.BlockSpec((2, tn), lambda i, j: (0, j))],
        out_specs=pl.BlockSpec((tm, tn), lambda i, j: (i, j)),
        scratch_shapes=[pltpu.VMEM((tm, d), BF16)],
        compiler_params=_params(("parallel", "arbitrary")),
        name="outproj",
    )(*ys, gb, w, x, gt)


def _ffn_norm_kernel(x_ref, g_ref, sh_ref, sc_ref, o_ref, *, n_lat):
    _norm_mod_rows(x_ref, g_ref, sh_ref, sc_ref, o_ref, pl.program_id(0) * x_ref.shape[0], n_lat)


def _ffn_norm(x, g, sh, sc, n_lat):
    r, d = x.shape
    tm = _pick(r, (640, 256))
    return pl.pallas_call(
        functools.partial(_ffn_norm_kernel, n_lat=n_lat),
        out_shape=jax.ShapeDtypeStruct((r, d), BF16),
        grid=(r // tm,),
        in_specs=[pl.BlockSpec((tm, d), lambda i: (i, 0)),
                  pl.BlockSpec((1, d), lambda i: (0, 0)),
                  pl.BlockSpec((2, d), lambda i: (0, 0)),
                  pl.BlockSpec((2, d), lambda i: (0, 0))],
        out_specs=pl.BlockSpec((tm, d), lambda i: (i, 0)),
        compiler_params=_params(("parallel",)),
        name="ffn_norm",
    )(x, g, sh, sc)


def _up_kernel(be_ref, nu_ref, x_ref, w1_ref, w3_ref, o_ref):
    b = pl.program_id(1)

    @pl.when(b < nu_ref[0])
    def _():
        x = x_ref[...].astype(BF16)
        a = jnp.dot(x, w1_ref[...], preferred_element_type=F32)
        c = jnp.dot(x, w3_ref[...], preferred_element_type=F32)
        o_ref[...] = (_silu(a) * c).astype(o_ref.dtype)

    @pl.when(b >= nu_ref[0])
    def _():
        o_ref[...] = jnp.zeros_like(o_ref)


def _swiglu_up(xb, blk_e, n_used, w1, w3, tb):
    rows, d = xb.shape
    f = w1.shape[2]
    tf = _pick(f, (1408, 512, 256, 128))
    nb = rows // tb
    live = lambda b, nu: jnp.minimum(b, nu[0] - 1)
    return pl.pallas_call(
        _up_kernel,
        out_shape=jax.ShapeDtypeStruct((rows, f), BF16),
        grid_spec=pltpu.PrefetchScalarGridSpec(
            num_scalar_prefetch=2,
            grid=(f // tf, nb),
            in_specs=[
                pl.BlockSpec((tb, d), lambda j, b, be, nu: (live(b, nu), 0)),
                pl.BlockSpec((None, d, tf), lambda j, b, be, nu: (be[live(b, nu)], 0, j)),
                pl.BlockSpec((None, d, tf), lambda j, b, be, nu: (be[live(b, nu)], 0, j)),
            ],
            out_specs=pl.BlockSpec((tb, tf), lambda j, b, be, nu: (b, j)),
        ),
        compiler_params=_params(("parallel", "arbitrary")),
        name="swiglu_up",
    )(blk_e, n_used, xb, w1, w3)


def _down_kernel(be_ref, nu_ref, g_ref, w2_ref, o_ref):
    b = pl.program_id(1)

    @pl.when(b < nu_ref[0])
    def _():
        o_ref[...] = jnp.dot(g_ref[...], w2_ref[...], preferred_element_type=F32)

    @pl.when(b >= nu_ref[0])
    def _():
        o_ref[...] = jnp.zeros_like(o_ref)


def _down_res_kernel(be_ref, nu_ref, g_ref, w2_ref, x_ref, gt_ref, o_ref, *, n_lat):
    tb = x_ref.shape[0]
    gate = _row_select(gt_ref, pl.program_id(1) * tb, tb, n_lat)
    o_ref[...] = x_ref[...] + gate * jnp.dot(g_ref[...], w2_ref[...], preferred_element_type=F32)


def _swiglu_down(gact, blk_e, n_used, w2, tb, resid=None):
    rows, f = gact.shape
    d = w2.shape[2]
    tn = _pick(d, (1024, 512, 256))
    nb = rows // tb
    live = lambda b, nu: jnp.minimum(b, nu[0] - 1)
    in_specs = [
        pl.BlockSpec((tb, f), lambda n, b, be, nu: (live(b, nu), 0)),
        pl.BlockSpec((None, f, tn), lambda n, b, be, nu: (be[live(b, nu)], 0, n)),
    ]
    args = [gact, w2]
    if resid is None:
        kern = _down_kernel
    else:
        x, gt, n_lat = resid
        kern = functools.partial(_down_res_kernel, n_lat=n_lat)
        in_specs += [pl.BlockSpec((tb, tn), lambda n, b, be, nu: (b, n)),
                     pl.BlockSpec((2, tn), lambda n, b, be, nu: (0, n))]
        args += [x, gt]
    return pl.pallas_call(
        kern,
        out_shape=jax.ShapeDtypeStruct((rows, d), F32),
        grid_spec=pltpu.PrefetchScalarGridSpec(
            num_scalar_prefetch=2,
            grid=(d // tn, nb),
            in_specs=in_specs,
            out_specs=pl.BlockSpec((tb, tn), lambda n, b, be, nu: (b, n)),
        ),
        compiler_params=_params(("parallel", "arbitrary")),
        name="swiglu_down",
    )(blk_e, n_used, *args)


def _dense_ffn(x, g2, sh, sc, gt, w1, w3, w2, n_lat):
    r = x.shape[0]
    tb = _pick(r, (640, 256))
    nb = r // tb
    blk_e = jnp.zeros((nb,), I32)
    n_used = jnp.full((1,), nb, I32)
    h = _ffn_norm(x, g2, sh, sc, n_lat)
    gact = _swiglu_up(h, blk_e, n_used, w1[None], w3[None], tb)
    return _swiglu_down(gact, blk_e, n_used, w2[None], tb, resid=(x, gt, n_lat))


def _route_kernel(x_ref, g_ref, sh_ref, sc_ref, wr_ref, br_ref, mi_ref, mf_ref, cnt_ref, h_s, run_s, *, n_lat, n_exp):
    i = pl.program_id(0)
    tm = x_ref.shape[0]

    @pl.when(i == 0)
    def _():
        run_s[...] = jnp.zeros_like(run_s)

    _norm_mod_rows(x_ref, g_ref, sh_ref, sc_ref, h_s, i * tm, n_lat)
    logits = jnp.dot(h_s[...], wr_ref[...], preferred_element_type=F32, precision=lax.Precision.HIGHEST) + br_ref[...]
    lane = lax.broadcasted_iota(I32, logits.shape, 1).astype(F32)
    l1 = jnp.where(lane < n_exp, logits, -jnp.inf)
    v1 = l1.max(axis=1, keepdims=True)
    e1 = jnp.where(l1 == v1, lane, float(V7X_LANES)).min(axis=1, keepdims=True)
    l2 = jnp.where(lane == e1, -jnp.inf, l1)
    v2 = l2.max(axis=1, keepdims=True)
    e2 = jnp.where(l2 == v2, lane, float(V7X_LANES)).min(axis=1, keepdims=True)
    t = jnp.exp(v2 - v1)
    g1 = 1.0 / (1.0 + t)
    g2 = t / (1.0 + t)

    onehot = jnp.where(jnp.logical_or(lane == e1, lane == e2), 1.0, 0.0)
    below = lax.broadcasted_iota(I32, (tm, tm), 0) > lax.broadcasted_iota(I32, (tm, tm), 1)
    before = jnp.dot(jnp.where(below, 1.0, 0.0).astype(BF16), onehot.astype(BF16),
                     preferred_element_type=F32) + run_s[...]
    r1 = jnp.where(lane == e1, before, 0.0).sum(axis=1, keepdims=True)
    r2 = jnp.where(lane == e2, before, 0.0).sum(axis=1, keepdims=True)
    run_s[...] = run_s[...] + onehot.sum(axis=0, keepdims=True)

    meta = jnp.where(lane == 0, e1, jnp.where(lane == 1, e2, jnp.where(lane == 2, r1, jnp.where(lane == 3, r2, 0.0))))
    mi_ref[...] = meta.astype(I32)
    mf_ref[...] = jnp.where(lane == 0, g1, jnp.where(lane == 1, g2, 0.0))
    cnt_ref[...] = jnp.broadcast_to(run_s[...], cnt_ref.shape)


def _route(x, g2, sh, sc, wr_pad, br_pad, n_lat, n_exp):
    r, d = x.shape
    tm = ROW_TILE
    return pl.pallas_call(
        functools.partial(_route_kernel, n_lat=n_lat, n_exp=n_exp),
        out_shape=(jax.ShapeDtypeStruct((r, V7X_LANES), I32),
                   jax.ShapeDtypeStruct((r, V7X_LANES), F32),
                   jax.ShapeDtypeStruct((V7X_SUBLANES, V7X_LANES), F32)),
        grid=(r // tm,),
        in_specs=[pl.BlockSpec((tm, d), lambda i: (i, 0)),
                  pl.BlockSpec((1, d), lambda i: (0, 0)),
                  pl.BlockSpec((2, d), lambda i: (0, 0)),
                  pl.BlockSpec((2, d), lambda i: (0, 0)),
                  pl.BlockSpec((d, V7X_LANES), lambda i: (0, 0)),
                  pl.BlockSpec((1, V7X_LANES), lambda i: (0, 0))],
        out_specs=(pl.BlockSpec((tm, V7X_LANES), lambda i: (i, 0)),
                   pl.BlockSpec((tm, V7X_LANES), lambda i: (i, 0)),
                   pl.BlockSpec((V7X_SUBLANES, V7X_LANES), lambda i: (0, 0))),
        scratch_shapes=[pltpu.VMEM((tm, d), F32), pltpu.VMEM((1, V7X_LANES), F32)],
        compiler_params=_params(("arbitrary",)),
        name="route",
    )(x, g2, sh, sc, wr_pad, br_pad)


def _row_copy(src, s_row, dst, d_row, sem):
    return pltpu.make_async_copy(src.at[pl.ds(s_row, 1), :], dst.at[pl.ds(d_row, 1), :], sem)


def _dispatch_kernel(d1_ref, d2_ref, x_ref, g_ref, sh_ref, sc_ref, xb_in_ref, xb_ref, h_s, sem, *, n_lat):
    del xb_in_ref
    i = pl.program_id(0)
    tm = x_ref.shape[0]
    _norm_mod_rows(x_ref, g_ref, sh_ref, sc_ref, h_s, i * tm, n_lat)

    def start(r, c):
        _row_copy(h_s, r, xb_ref, d1_ref[0, r], sem.at[0]).start()
        _row_copy(h_s, r, xb_ref, d2_ref[0, r], sem.at[1]).start()
        return c

    def wait(r, c):
        _row_copy(h_s, r, xb_ref, d1_ref[0, r], sem.at[0]).wait()
        _row_copy(h_s, r, xb_ref, d2_ref[0, r], sem.at[1]).wait()
        return c

    lax.fori_loop(0, tm, start, 0)
    lax.fori_loop(0, tm, wait, 0)


def _dispatch(x, g2, sh, sc, d1, d2, n_rows, n_lat):
    r, d = x.shape
    tm = ROW_TILE
    nt = r // tm
    smem_rows = pl.BlockSpec((None, 1, tm), lambda i: (i, 0, 0), memory_space=pltpu.SMEM)
    return pl.pallas_call(
        functools.partial(_dispatch_kernel, n_lat=n_lat),
        out_shape=jax.ShapeDtypeStruct((n_rows, d), F32),
        grid=(nt,),
        in_specs=[smem_rows, smem_rows,
                  pl.BlockSpec((tm, d), lambda i: (i, 0)),
                  pl.BlockSpec((1, d), lambda i: (0, 0)),
                  pl.BlockSpec((2, d), lambda i: (0, 0)),
                  pl.BlockSpec((2, d), lambda i: (0, 0)),
                  pl.BlockSpec(memory_space=pl.ANY)],
        out_specs=pl.BlockSpec(memory_space=pl.ANY),
        scratch_shapes=[pltpu.VMEM((tm, d), F32), pltpu.SemaphoreType.DMA((2,))],
        input_output_aliases={6: 0},
        compiler_params=_params(("arbitrary",)),
        name="moe_dispatch",
    )(d1.reshape(nt, 1, tm), d2.reshape(nt, 1, tm), x, g2, sh, sc, jnp.zeros((n_rows, d), F32))


def _combine_kernel(d1_ref, d2_ref, mf_ref, x_ref, gt_ref, yb_ref, o_ref, buf, sem, *, n_lat):
    i = pl.program_id(0)
    tm = x_ref.shape[0]

    def start(r, c):
        _row_copy(yb_ref, d1_ref[0, r], buf.at[0], r, sem.at[0]).start()
        _row_copy(yb_ref, d2_ref[0, r], buf.at[1], r, sem.at[1]).start()
        return c

    def wait(r, c):
        _row_copy(yb_ref, d1_ref[0, r], buf.at[0], r, sem.at[0]).wait()
        _row_copy(yb_ref, d2_ref[0, r], buf.at[1], r, sem.at[1]).wait()
        return c

    lax.fori_loop(0, tm, start, 0)
    lax.fori_loop(0, tm, wait, 0)
    f = mf_ref[:, 0:1] * buf[0] + mf_ref[:, 1:2] * buf[1]
    o_ref[...] = x_ref[...] + _row_select(gt_ref, i * tm, tm, n_lat) * f


def _combine(x, gt, yb, d1, d2, mf, n_lat):
    r, d = x.shape
    tm = ROW_TILE
    nt = r // tm
    smem_rows = pl.BlockSpec((None, 1, tm), lambda i: (i, 0, 0), memory_space=pltpu.SMEM)
    return pl.pallas_call(
        functools.partial(_combine_kernel, n_lat=n_lat),
        out_shape=jax.ShapeDtypeStruct((r, d), F32),
        grid=(nt,),
        in_specs=[smem_rows, smem_rows,
                  pl.BlockSpec((tm, V7X_LANES), lambda i: (i, 0)),
                  pl.BlockSpec((tm, d), lambda i: (i, 0)),
                  pl.BlockSpec((2, d), lambda i: (0, 0)),
                  pl.BlockSpec(memory_space=pl.ANY)],
        out_specs=pl.BlockSpec((tm, d), lambda i: (i, 0)),
        scratch_shapes=[pltpu.VMEM((2, tm, d), F32), pltpu.SemaphoreType.DMA((2,))],
        compiler_params=_params(("arbitrary",)),
        name="moe_combine",
    )(d1.reshape(nt, 1, tm), d2.reshape(nt, 1, tm), mf, x, gt, yb)


def _moe_ffn(x, g2, sh, sc, gt, wr, br, w1, w3, w2, n_lat):
    r, d = x.shape
    n_exp = wr.shape[1]
    blk = MOE_BLOCK
    wr_pad = jnp.zeros((d, V7X_LANES), F32).at[:, :n_exp].set(wr)
    br_pad = jnp.zeros((1, V7X_LANES), F32).at[0, :n_exp].set(br)
    mi, mf, cnt = _route(x, g2, sh, sc, wr_pad, br_pad, n_lat, n_exp)

    counts = cnt[0, :n_exp].astype(I32)
    padded = (counts + blk - 1) // blk * blk
    pad_end = jnp.cumsum(padded)
    pad_start = pad_end - padded
    d1 = pad_start[mi[:, 0]] + mi[:, 2]
    d2 = pad_start[mi[:, 1]] + mi[:, 3]
    n_rows = -(-(TOP_K * r + n_exp * (blk - 1)) // blk) * blk
    nb = n_rows // blk
    blk_e = jnp.minimum(jnp.searchsorted(pad_end, jnp.arange(nb, dtype=I32) * blk, side="right"),
                        n_exp - 1).astype(I32)
    n_used = (pad_end[-1:] // blk).astype(I32)

    xb = _dispatch(x, g2, sh, sc, d1, d2, n_rows, n_lat)
    gact = _swiglu_up(xb, blk_e, n_used, w1, w3, blk)
    yb = _swiglu_down(gact, blk_e, n_used, w2, blk)
    return _combine(x, gt, yb, d1, d2, mf, n_lat)


def _rope_tables(n_lat, n_ctx):
    n_rows = n_lat // GRID_W
    row = jnp.repeat(jnp.arange(n_rows, dtype=F32), GRID_W)
    col = jnp.tile(jnp.arange(GRID_W, dtype=F32), n_rows)
    axis_dim = HEAD_DIM // 2
    inv_freq = ROPE_THETA ** (-jnp.arange(0, axis_dim, 2, dtype=F32) / axis_dim)
    ang_r = row[:, None] * inv_freq[None, :]
    ang_c = col[:, None] * inv_freq[None, :]
    cr, sr, cc, sc = jnp.cos(ang_r), jnp.sin(ang_r), jnp.cos(ang_c), jnp.sin(ang_c)
    cos_t = jnp.concatenate([cr, cr, cc, cc], axis=1)
    sin_t = jnp.concatenate([-sr, sr, -sc, sc], axis=1)
    cos_t = jnp.concatenate([cos_t, jnp.ones((n_ctx, HEAD_DIM), F32)], axis=0)
    sin_t = jnp.concatenate([sin_t, jnp.zeros((n_ctx, HEAD_DIM), F32)], axis=0)
    return cos_t, sin_t


def kernel(x, c, ctx, c_ctx, w_ada, b_ada, g_norm1, w_in, conv_w, conv_b, conv_ln_g, conv_ln_b, conv_pw, conv_pw_b, sgu_ln_g, sgu_ln_b, sgu_w, sgu_b, swa_q_g, swa_k_g, swa_sink, glb_q_g, glb_k_g, g_branch, w_out, g_norm2, ffn_w1, ffn_w3, ffn_w2, router_w, router_b, exp_w1, exp_w3, exp_w2):
    batch, n_lat, d = x.shape
    n_ctx = ctx.shape[1]
    depth = w_ada.shape[0]
    gw = d // N_GROUPS
    assert batch == 1 and n_lat % n_ctx == 0 and n_ctx % ROW_TILE == 0 and n_lat % GRID_W == 0
    assert conv_w.shape[1] // 2 < CONV_HALO and w_in.shape[2] == 4 * 2 * gw

    xs = jnp.concatenate([x[0], ctx[0]], axis=0)
    mods = _ada(jnp.stack([c[0], c_ctx], axis=1), w_ada, b_ada)
    cos_t, sin_t = _rope_tables(n_lat, n_ctx)
    row2 = lambda v: v.reshape(1, -1)

    for l in range(depth):
        sh1, sc1, gt1, sh2, sc2, gt2 = (mods[l, :, k * d:(k + 1) * d] for k in range(6))

        p = _inproj(xs, row2(g_norm1[l]), sh1, sc1, w_in[l].astype(BF16), n_lat)
        y_conv = _conv_group(p, conv_w[l], row2(conv_b[l]), row2(conv_ln_g[l]), row2(conv_ln_b[l]),
                             conv_pw[l].astype(BF16), row2(conv_pw_b[l]), n_lat)
        y_sgu = _sgu_group(p, row2(sgu_ln_g[l]), row2(sgu_ln_b[l]), sgu_w[l].astype(BF16), sgu_b[l].T)
        q, k, v = _prep(p, jnp.stack([swa_q_g[l], glb_q_g[l]])[:, None, :],
                        jnp.stack([swa_k_g[l], glb_k_g[l]])[:, None, :], cos_t, sin_t, first_block=2)
        y_swa = _swa(q, k, v, swa_sink[l], n_lat, grp=0)
        y_glb = _glb(q, k, v, n_lat, grp=1)
        xs = _outproj((y_conv, y_sgu, y_swa, y_glb), row2(g_branch[l]), w_out[l].astype(BF16), xs, gt1, n_lat)

        if l % 2 == 0:
            xs = _dense_ffn(xs, row2(g_norm2[l]), sh2, sc2, gt2, ffn_w1[l // 2].astype(BF16),
                            ffn_w3[l // 2].astype(BF16), ffn_w2[l // 2].astype(BF16), n_lat)
        else:
            xs = _moe_ffn(xs, row2(g_norm2[l]), sh2, sc2, gt2, router_w[l // 2], router_b[l // 2],
                          exp_w1[l // 2].astype(BF16), exp_w3[l // 2].astype(BF16), exp_w2[l // 2].astype(BF16), n_lat)
    return xs[:n_lat][None]
```

```python
import functools
import math

import jax
import jax.numpy as jnp
from jax import lax
from jax.experimental import pallas as pl
from jax.experimental.pallas import tpu as pltpu

F32 = jnp.float32
BF16 = jnp.bfloat16
I32 = jnp.int32

HEAD_DIM = 128
N_GROUPS = 4
N_KV_HEADS = 2
GRID_W = 64
WINDOW = 128
ROPE_THETA = 10000.0
TOP_K = 2
MOE_BLOCK = 512
EPS = 1e-6
NEG = -1e30
SCALE = HEAD_DIM ** -0.5
Q_PRESCALE = SCALE * math.log2(math.e)

V7X_VMEM_BYTES = 64 * 1024 * 1024
V7X_LANES = 128
V7X_SUBLANES = 8
VMEM_BUDGET = V7X_VMEM_BYTES - 8 * 1024 * 1024

ROW_TILE = 256
CONV_HALO = 16
NORM_CHUNK = 128


def _pick(n, cands):
    for c in cands:
        if n % c == 0:
            return c
    raise ValueError(f"no tile in {cands} divides {n}")


def _params(sem, vmem=VMEM_BUDGET):
    return pltpu.CompilerParams(dimension_semantics=sem, vmem_limit_bytes=vmem)


def _sigmoid(x):
    return 1.0 / (1.0 + jnp.exp(-x))


def _silu(x):
    return x * _sigmoid(x)


def _gelu_tanh(x):
    c = math.sqrt(2.0 / math.pi)
    return 0.5 * x * (1.0 + jnp.tanh(c * (x + 0.044715 * (x * x * x))))


def _rms(x):
    return x * lax.rsqrt(jnp.mean(x * x, axis=-1, keepdims=True) + EPS)


def _layer_norm(x, g, b):
    xc = x - jnp.mean(x, axis=-1, keepdims=True)
    return xc * lax.rsqrt(jnp.mean(xc * xc, axis=-1, keepdims=True) + EPS) * g + b


def _row_select(vec2_ref, row0, tm, n_lat):
    rows = row0 + lax.broadcasted_iota(I32, (tm, 1), 0)
    return jnp.where(rows >= n_lat, vec2_ref[1:2, :], vec2_ref[0:1, :])


def _norm_mod(x, g, shift, scale):
    return _rms(x) * g * (1.0 + scale) + shift


def _norm_mod_rows(x_ref, g_ref, sh_ref, sc_ref, out_ref, row0, n_lat):
    tm = x_ref.shape[0]

    def body(c, carry):
        r0 = pl.multiple_of(c * NORM_CHUNK, NORM_CHUNK)
        rows = pl.ds(r0, NORM_CHUNK)
        sh = _row_select(sh_ref, row0 + r0, NORM_CHUNK, n_lat)
        sc = _row_select(sc_ref, row0 + r0, NORM_CHUNK, n_lat)
        out_ref[rows, :] = _norm_mod(x_ref[rows, :], g_ref[...], sh, sc).astype(out_ref.dtype)
        return carry

    lax.fori_loop(0, tm // NORM_CHUNK, body, 0)


def _ada_kernel(s_ref, w_ref, b_ref, o_ref):
    d, tn = w_ref.shape
    kc = 64

    def body(c, acc):
        a0, a1 = acc
        k0 = pl.multiple_of(c * kc, kc)
        w = w_ref[pl.ds(k0, kc), :]
        s = _silu(s_ref[pl.ds(k0, kc), :])
        a0 = a0 + (w * s[:, 0:1]).reshape(kc // V7X_SUBLANES, V7X_SUBLANES, tn).sum(axis=0)
        a1 = a1 + (w * s[:, 1:2]).reshape(kc // V7X_SUBLANES, V7X_SUBLANES, tn).sum(axis=0)
        return a0, a1

    z = jnp.zeros((V7X_SUBLANES, tn), F32)
    a0, a1 = lax.fori_loop(0, d // kc, body, (z, z))
    o_ref[0:1, :] = a0.sum(axis=0, keepdims=True) + b_ref[...]
    o_ref[1:2, :] = a1.sum(axis=0, keepdims=True) + b_ref[...]


def _ada(cond, w_ada, b_ada):
    depth, d, n = w_ada.shape
    tn = _pick(n, (1024, 512, 256, 128))
    return pl.pallas_call(
        _ada_kernel,
        out_shape=jax.ShapeDtypeStruct((depth, 2, n), F32),
        grid=(depth, n // tn),
        in_specs=[
            pl.BlockSpec((d, 2), lambda l, j: (0, 0)),
            pl.BlockSpec((None, d, tn), lambda l, j: (l, 0, j)),
            pl.BlockSpec((None, 1, tn), lambda l, j: (l, 0, j)),
        ],
        out_specs=pl.BlockSpec((None, 2, tn), lambda l, j: (l, 0, j)),
        compiler_params=_params(("parallel", "parallel")),
        name="ada",
    )(cond, w_ada, b_ada.reshape(depth, 1, n))


def _inproj_kernel(x_ref, g_ref, sh_ref, sc_ref, w_ref, o_ref, h_s, *, n_lat):
    i = pl.program_id(0)
    tm = x_ref.shape[0]

    @pl.when(pl.program_id(1) == 0)
    def _():
        _norm_mod_rows(x_ref, g_ref, sh_ref, sc_ref, h_s, i * tm, n_lat)

    o_ref[...] = jnp.dot(h_s[...], w_ref[...], preferred_element_type=F32)


def _inproj(x, g, sh, sc, w, n_lat):
    r, d = x.shape
    n = w.shape[1]
    tm = _pick(r, (640, 256))
    tn = _pick(n, (1024, 512, 256))
    return pl.pallas_call(
        functools.partial(_inproj_kernel, n_lat=n_lat),
        out_shape=jax.ShapeDtypeStruct((r, n), F32),
        grid=(r // tm, n // tn),
        in_specs=[
            pl.BlockSpec((tm, d), lambda i, j: (i, 0)),
            pl.BlockSpec((1, d), lambda i, j: (0, 0)),
            pl.BlockSpec((2, d), lambda i, j: (0, 0)),
            pl.BlockSpec((2, d), lambda i, j: (0, 0)),
            pl.BlockSpec((d, tn), lambda i, j: (0, j)),
        ],
        out_specs=pl.BlockSpec((tm, tn), lambda i, j: (i, j)),
        scratch_shapes=[pltpu.VMEM((tm, d), BF16)],
        compiler_params=_params(("parallel", "arbitrary")),
        name="inproj",
    )(x, g, sh, sc, w)


def _conv_kernel(pm_ref, pp_ref, pn_ref, cw_ref, cb_ref, lg_ref, lb_ref, pw_ref, pb_ref, o_ref, ext_s,
                 *, n_lat_tiles, n_tiles):
    i = pl.program_id(0)
    tm, gw = o_ref.shape
    kw = cw_ref.shape[0]

    def glu(p):
        return p[:, :gw] * _sigmoid(p[:, gw:])

    prev_ok = jnp.logical_and(i != 0, i != n_lat_tiles)
    next_ok = jnp.logical_and(i != n_lat_tiles - 1, i != n_tiles - 1)
    ext_s[0:CONV_HALO, :] = jnp.where(prev_ok, glu(pp_ref[...]), 0.0)
    ext_s[CONV_HALO:CONV_HALO + tm, :] = glu(pm_ref[...])
    ext_s[CONV_HALO + tm:, :] = jnp.where(next_ok, glu(pn_ref[...]), 0.0)

    acc = jnp.zeros((tm, gw), F32) + cb_ref[...]
    for k in range(kw):
        off = CONV_HALO - kw // 2 + k
        acc = acc + cw_ref[k:k + 1, :] * ext_s[off:off + tm, :]
    y = _silu(_layer_norm(acc, lg_ref[...], lb_ref[...]))
    o_ref[...] = jnp.dot(y.astype(BF16), pw_ref[...], preferred_element_type=F32) + pb_ref[...]


def _conv_group(p, cw, cb, lg, lb, pw, pb, n_lat):
    r = p.shape[0]
    kw, gw = cw.shape
    tm = ROW_TILE
    n_tiles = r // tm
    hb = tm // CONV_HALO
    last_hb = r // CONV_HALO - 1
    return pl.pallas_call(
        functools.partial(_conv_kernel, n_lat_tiles=n_lat // tm, n_tiles=n_tiles),
        out_shape=jax.ShapeDtypeStruct((r, gw), F32),
        grid=(n_tiles,),
        in_specs=[
            pl.BlockSpec((tm, 2 * gw), lambda i: (i, 0)),
            pl.BlockSpec((CONV_HALO, 2 * gw), lambda i: (jnp.maximum(i * hb - 1, 0), 0)),
            pl.BlockSpec((CONV_HALO, 2 * gw), lambda i: (jnp.minimum((i + 1) * hb, last_hb), 0)),
            pl.BlockSpec((kw, gw), lambda i: (0, 0)),
            pl.BlockSpec((1, gw), lambda i: (0, 0)),
            pl.BlockSpec((1, gw), lambda i: (0, 0)),
            pl.BlockSpec((1, gw), lambda i: (0, 0)),
            pl.BlockSpec((gw, gw), lambda i: (0, 0)),
            pl.BlockSpec((1, gw), lambda i: (0, 0)),
        ],
        out_specs=pl.BlockSpec((tm, gw), lambda i: (i, 0)),
        scratch_shapes=[pltpu.VMEM((tm + 2 * CONV_HALO, gw), F32)],
        compiler_params=_params(("parallel",)),
        name="conv_group",
    )(p, p, p, cw, cb, lg, lb, pw, pb)


def _sgu_kernel(p_ref, lg_ref, lb_ref, ws_ref, bs_ref, o_ref):
    tm, gw = o_ref.shape
    n_h, ch, _ = ws_ref.shape
    hd = gw // n_h
    z = _gelu_tanh(p_ref[...])
    u = z[:, :gw]
    v = _layer_norm(z[:, gw:], lg_ref[...], lb_ref[...]).astype(BF16)
    for c in range(tm // ch):
        rows = slice(c * ch, (c + 1) * ch)
        parts = []
        for h in range(n_h):
            s = jnp.dot(ws_ref[h], v[rows, h * hd:(h + 1) * hd], preferred_element_type=F32)
            parts.append(s + bs_ref[:, h:h + 1])
        o_ref[rows, :] = u[rows, :] * jnp.concatenate(parts, axis=1)


def _sgu_group(p, lg, lb, ws, bs_t):
    r = p.shape[0]
    gw = lg.shape[1]
    n_h, ch, _ = ws.shape
    tm = ROW_TILE
    return pl.pallas_call(
        _sgu_kernel,
        out_shape=jax.ShapeDtypeStruct((r, gw), F32),
        grid=(r // tm,),
        in_specs=[
            pl.BlockSpec((tm, 2 * gw), lambda i: (i, 1)),
            pl.BlockSpec((1, gw), lambda i: (0, 0)),
            pl.BlockSpec((1, gw), lambda i: (0, 0)),
            pl.BlockSpec((n_h, ch, ch), lambda i: (0, 0, 0)),
            pl.BlockSpec((ch, n_h), lambda i: (0, 0)),
        ],
        out_specs=pl.BlockSpec((tm, gw), lambda i: (i, 0)),
        compiler_params=_params(("parallel",)),
        name="sgu_group",
    )(p, lg, lb, ws, bs_t)


def _prep_kernel(p_ref, qg_ref, kg_ref, cos_ref, sin_ref, q_ref, k_ref, v_ref, qt_ref, vt_ref):
    tm = p_ref.shape[0]
    qw, kvw = q_ref.shape[1], k_ref.shape[1]
    cos = cos_ref[...]
    sin = sin_ref[...]
    lane = lax.broadcasted_iota(I32, (tm, HEAD_DIM), 1)
    low = (lane & (HEAD_DIM // 4)) == 0

    def head(x, g, mult):
        y = _rms(x) * g
        partner = jnp.where(low, pltpu.roll(y, HEAD_DIM - HEAD_DIM // 4, 1), pltpu.roll(y, HEAD_DIM // 4, 1))
        return (y * cos + partner * sin) * mult

    for h in range(qw // HEAD_DIM):
        cols = slice(h * HEAD_DIM, (h + 1) * HEAD_DIM)
        qh = head(p_ref[:, cols], qg_ref[...], Q_PRESCALE)
        q_ref[:, cols] = qh.astype(BF16)
        qt_ref[cols, :] = qh.T.astype(BF16)
    for h in range(kvw // HEAD_DIM):
        cols = slice(h * HEAD_DIM, (h + 1) * HEAD_DIM)
        k_ref[:, cols] = head(p_ref[:, qw + h * HEAD_DIM:qw + (h + 1) * HEAD_DIM], kg_ref[...], 1.0).astype(BF16)
        vh = p_ref[:, qw + kvw + h * HEAD_DIM:qw + kvw + (h + 1) * HEAD_DIM]
        v_ref[:, cols] = vh.astype(BF16)
        vt_ref[cols, :] = vh.T.astype(BF16)


def _prep(p, qg, kg, cos_t, sin_t, first_block):
    r = p.shape[0]
    n_att = qg.shape[0]
    qw = N_GROUPS * HEAD_DIM
    kvw = N_KV_HEADS * HEAD_DIM
    tm = ROW_TILE
    rows = lambda w: pl.BlockSpec((None, tm, w), lambda i, a: (a, i, 0))
    cols = lambda w: pl.BlockSpec((None, w, tm), lambda i, a: (a, 0, i))
    return pl.pallas_call(
        _prep_kernel,
        out_shape=(jax.ShapeDtypeStruct((n_att, r, qw), BF16),
                   jax.ShapeDtypeStruct((n_att, r, kvw), BF16),
                   jax.ShapeDtypeStruct((n_att, r, kvw), BF16),
                   jax.ShapeDtypeStruct((n_att, qw, r), BF16),
                   jax.ShapeDtypeStruct((n_att, kvw, r), BF16)),
        grid=(r // tm, n_att),
        in_specs=[
            pl.BlockSpec((tm, qw + 2 * kvw), lambda i, a: (i, first_block + a)),
            pl.BlockSpec((None, 1, HEAD_DIM), lambda i, a: (a, 0, 0)),
            pl.BlockSpec((None, 1, HEAD_DIM), lambda i, a: (a, 0, 0)),
            pl.BlockSpec((tm, HEAD_DIM), lambda i, a: (i, 0)),
            pl.BlockSpec((tm, HEAD_DIM), lambda i, a: (i, 0)),
        ],
        out_specs=(rows(qw), rows(kvw), rows(kvw), cols(qw), cols(kvw)),
        compiler_params=_params(("parallel", "parallel")),
        name="qkv_prep",
    )(p, qg, kg, cos_t, sin_t)


def _nt_dot(a, b):
    return lax.dot_general(a, b, (((1,), (1,)), ((), ())), preferred_element_type=F32)


def _swa_kernel(sink_ref, q_ref, km_ref, kp_ref, kn_ref, vm_ref, vp_ref, vn_ref, kc_ref, vc_ref, o_ref,
                *, n_lat):
    i = pl.program_id(0)
    tq = q_ref.shape[0]
    hb = kp_ref.shape[0]
    n_rep = (q_ref.shape[1] // HEAD_DIM) // N_KV_HEADS
    nk = tq + 2 * hb
    qi = lax.broadcasted_iota(I32, (tq, nk), 0)
    kj = lax.broadcasted_iota(I32, (tq, nk), 1) - hb
    kglob = i * tq + kj
    ok1 = (jnp.abs(kj - qi) <= WINDOW) & (kglob >= 0) & (kglob < n_lat) & (i * tq < n_lat)
    ok = jnp.concatenate([ok1.astype(F32)] * n_rep, axis=0) > 0.5
    rep_of_row = jnp.concatenate([jnp.full((tq, 1), g, I32) for g in range(n_rep)], axis=0)
    for j in range(N_KV_HEADS):
        kv = slice(j * HEAD_DIM, (j + 1) * HEAD_DIM)
        q2 = jnp.concatenate(
            [q_ref[:, (j * n_rep + g) * HEAD_DIM:(j * n_rep + g + 1) * HEAD_DIM] for g in range(n_rep)], axis=0)
        kw = jnp.concatenate([kp_ref[:, kv], km_ref[:, kv], kn_ref[:, kv]], axis=0)
        vw = jnp.concatenate([vp_ref[:, kv], vm_ref[:, kv], vn_ref[:, kv]], axis=0)
        s_w = jnp.where(ok, _nt_dot(q2, kw), NEG)
        s_c = _nt_dot(q2, kc_ref[:, kv])
        sk = jnp.zeros((n_rep * tq, 1), F32)
        for g in range(n_rep):
            sk = jnp.where(rep_of_row == g, sink_ref[j * n_rep + g] * math.log2(math.e), sk)
        m = jnp.maximum(jnp.maximum(s_w.max(axis=1, keepdims=True), s_c.max(axis=1, keepdims=True)), sk)
        p_w = jnp.exp2(s_w - m)
        p_c = jnp.exp2(s_c - m)
        den = jnp.exp2(sk - m) + p_w.sum(axis=1, keepdims=True) + p_c.sum(axis=1, keepdims=True)
        o = (jnp.dot(p_w.astype(BF16), vw, preferred_element_type=F32)
             + jnp.dot(p_c.astype(BF16), vc_ref[:, kv], preferred_element_type=F32)) / den
        for g in range(n_rep):
            h = j * n_rep + g
            o_ref[:, h * HEAD_DIM:(h + 1) * HEAD_DIM] = o[g * tq:(g + 1) * tq, :]


def _swa(q, k, v, sink, n_lat, grp):
    _, r, qw = q.shape
    kvw = k.shape[2]
    n_ctx = r - n_lat
    tq = ROW_TILE
    hb = WINDOW
    per = tq // hb
    last_hb = r // hb - 1
    main = lambda i: (grp, i, 0)
    prev = lambda i: (grp, jnp.maximum(i * per - 1, 0), 0)
    nxt = lambda i: (grp, jnp.minimum((i + 1) * per, last_hb), 0)
    ctx = lambda i: (grp, n_lat // n_ctx, 0)
    return pl.pallas_call(
        functools.partial(_swa_kernel, n_lat=n_lat),
        out_shape=jax.ShapeDtypeStruct((r, qw), F32),
        grid=(r // tq,),
        in_specs=[
            pl.BlockSpec(memory_space=pltpu.SMEM),
            pl.BlockSpec((None, tq, qw), main),
            pl.BlockSpec((None, tq, kvw), main),
            pl.BlockSpec((None, hb, kvw), prev),
            pl.BlockSpec((None, hb, kvw), nxt),
            pl.BlockSpec((None, tq, kvw), main),
            pl.BlockSpec((None, hb, kvw), prev),
            pl.BlockSpec((None, hb, kvw), nxt),
            pl.BlockSpec((None, n_ctx, kvw), ctx),
            pl.BlockSpec((None, n_ctx, kvw), ctx),
        ],
        out_specs=pl.BlockSpec((tq, qw), lambda i: (i, 0)),
        compiler_params=_params(("parallel",)),
        name="window_attn",
    )(sink, q, k, k, k, v, v, v, k, v)


def _glb_kernel(qt_ref, k_ref, vt_ref, o_ref, sa_s, sb_s, acc_s, *, n_lat, tk):
    i = pl.program_id(1)
    tq = qt_ref.shape[1]
    n_rep = qt_ref.shape[0] // HEAD_DIM
    nq = n_rep * tq
    n_ctx = k_ref.shape[0] - n_lat
    q2t = jnp.concatenate([qt_ref[g * HEAD_DIM:(g + 1) * HEAD_DIM, :] for g in range(n_rep)], axis=1)

    def scores(k0, size):
        return jnp.dot(k_ref[pl.ds(k0, size), :], q2t, preferred_element_type=F32)

    def update(s, vt, m, l):
        m_new = jnp.maximum(m, s.max(axis=0, keepdims=True))
        alpha = jnp.exp2(m - m_new)
        p = jnp.exp2(s - m_new)
        l = alpha * l + p.sum(axis=0, keepdims=True)
        acc_s[...] = alpha * acc_s[...] + jnp.dot(vt, p.astype(BF16), preferred_element_type=F32)
        return m_new, l

    acc_s[...] = jnp.zeros_like(acc_s)
    m0, l0 = update(scores(n_lat, n_ctx), vt_ref[:, n_lat:],
                    jnp.full((1, nq), NEG, F32), jnp.zeros((1, nq), F32))
    last = n_lat // tk - 1

    def latent_keys():
        sa_s[...] = scores(0, tk)

        def body(c, carry):
            m, l = carry
            ka = pl.multiple_of(2 * c * tk, tk)
            kb = pl.multiple_of((2 * c + 1) * tk, tk)
            kn = pl.multiple_of(jnp.minimum(2 * c + 2, last) * tk, tk)
            sb_s[...] = scores(kb, tk)
            m, l = update(sa_s[...], vt_ref[:, pl.ds(ka, tk)], m, l)
            sa_s[...] = scores(kn, tk)
            return update(sb_s[...], vt_ref[:, pl.ds(kb, tk)], m, l)

        return lax.fori_loop(0, (n_lat // tk) // 2, body, (m0, l0))

    _, l = lax.cond(i * tq < n_lat, latent_keys, lambda: (m0, l0))
    o = (acc_s[...] / l).T
    for g in range(n_rep):
        o_ref[:, g * HEAD_DIM:(g + 1) * HEAD_DIM] = o[g * tq:(g + 1) * tq, :]


def _glb(qt, k, vt, n_lat, grp):
    _, r, kvw = k.shape
    qw = qt.shape[1]
    tq = ROW_TILE
    gq = qw // N_KV_HEADS
    tk = _pick(n_lat, (1024, 512, 256))
    tk = tk // 2
    nq = gq // HEAD_DIM * tq
    return pl.pallas_call(
        functools.partial(_glb_kernel, n_lat=n_lat, tk=tk),
        out_shape=jax.ShapeDtypeStruct((r, qw), F32),
        grid=(N_KV_HEADS, r // tq),
        in_specs=[
            pl.BlockSpec((None, gq, tq), lambda j, i: (grp, j, i)),
            pl.BlockSpec((None, r, HEAD_DIM), lambda j, i: (grp, 0, j)),
            pl.BlockSpec((None, HEAD_DIM, r), lambda j, i: (grp, j, 0)),
        ],
        out_specs=pl.BlockSpec((tq, gq), lambda j, i: (i, j)),
        scratch_shapes=[pltpu.VMEM((tk, nq), F32), pltpu.VMEM((tk, nq), F32), pltpu.VMEM((HEAD_DIM, nq), F32)],
        compiler_params=_params(("parallel", "parallel")),
        name="global_attn",
    )(qt, k, vt)


def _outproj_kernel(y0_ref, y1_ref, y2_ref, y3_ref, gb_ref, w_ref, x_ref, gt_ref, o_ref, yn_s, *, n_lat):
    i = pl.program_id(0)
    tm = x_ref.shape[0]

    @pl.when(pl.program_id(1) == 0)
    def _():
        for g, y_ref in enumerate((y0_ref, y1_ref, y2_ref, y3_ref)):
            gw = y_ref.shape[1]
            cols = slice(g * gw, (g + 1) * gw)
            yn_s[:, cols] = (_rms(y_ref[...]) * gb_ref[:, cols]).astype(BF16)

    gate = _row_select(gt_ref, i * tm, tm, n_lat)
    o_ref[...] = x_ref[...] + gate * jnp.dot(yn_s[...], w_ref[...], preferred_element_type=F32)


def _outproj(ys, gb, w, x, gt, n_lat):
    r, d = x.shape
    gw = ys[0].shape[1]
    tm = _pick(r, (640, 256))
    tn = _pick(d, (1024, 512, 256))
    ysp = pl.BlockSpec((tm, gw), lambda i, j: (i, 0))
    return pl.pallas_call(
        functools.partial(_outproj_kernel, n_lat=n_lat),
        out_shape=jax.ShapeDtypeStruct((r, d), F32),
        grid=(r // tm, d // tn),
        in_specs=[ysp, ysp, ysp, ysp,
                  pl.BlockSpec((1, d), lambda i, j: (0, 0)),
                  pl.BlockSpec((d, tn), lambda i, j: (0, j)),
                  pl.BlockSpec((tm, tn), lambda i, j: (i, j)),
                  pl.BlockSpec((2, tn), lambda i, j: (0, j))],
        out_specs=pl.BlockSpec((tm, tn), lambda i, j: (i, j)),
        scratch_shapes=[pltpu.VMEM((tm, d), BF16)],
        compiler_params=_params(("parallel", "arbitrary")),
        name="outproj",
    )(*ys, gb, w, x, gt)


def _ffn_norm_kernel(x_ref, g_ref, sh_ref, sc_ref, o_ref, *, n_lat):
    _norm_mod_rows(x_ref, g_ref, sh_ref, sc_ref, o_ref, pl.program_id(0) * x_ref.shape[0], n_lat)


def _ffn_norm(x, g, sh, sc, n_lat):
    r, d = x.shape
    tm = _pick(r, (640, 256))
    return pl.pallas_call(
        functools.partial(_ffn_norm_kernel, n_lat=n_lat),
        out_shape=jax.ShapeDtypeStruct((r, d), BF16),
        grid=(r // tm,),
        in_specs=[pl.BlockSpec((tm, d), lambda i: (i, 0)),
                  pl.BlockSpec((1, d), lambda i: (0, 0)),
                  pl.BlockSpec((2, d), lambda i: (0, 0)),
                  pl.BlockSpec((2, d), lambda i: (0, 0))],
        out_specs=pl.BlockSpec((tm, d), lambda i: (i, 0)),
        compiler_params=_params(("parallel",)),
        name="ffn_norm",
    )(x, g, sh, sc)


def _up_kernel(be_ref, nu_ref, x_ref, w1_ref, w3_ref, o_ref):
    b = pl.program_id(1)

    @pl.when(b < nu_ref[0])
    def _():
        x = x_ref[...].astype(BF16)
        a = jnp.dot(x, w1_ref[...].astype(BF16), preferred_element_type=F32)
        c = jnp.dot(x, w3_ref[...].astype(BF16), preferred_element_type=F32)
        o_ref[...] = (_silu(a) * c).astype(o_ref.dtype)

    @pl.when(b >= nu_ref[0])
    def _():
        o_ref[...] = jnp.zeros_like(o_ref)


def _fit_tile(n, cands, vmem_bytes):
    for c in cands:
        if n % c == 0 and vmem_bytes(c) <= VMEM_BUDGET:
            return c
    raise ValueError(f"no tile in {cands} divides {n} within the VMEM budget")


def _swiglu_up(xb, blk_e, n_used, w1, w3, layer, tb):
    rows, d = xb.shape
    f = w1.shape[3]
    wb, xbytes = w1.dtype.itemsize, xb.dtype.itemsize
    tf = _fit_tile(f, (1408, 1024, 512, 256, 128),
                   lambda t: 2 * tb * d * xbytes + 4 * d * t * wb + 4 * tb * t + 2 * tb * d + 4 * d * t + 16 * tb * t)
    nb = rows // tb
    live = lambda b, nu: jnp.minimum(b, nu[0] - 1)
    return pl.pallas_call(
        _up_kernel,
        out_shape=jax.ShapeDtypeStruct((rows, f), BF16),
        grid_spec=pltpu.PrefetchScalarGridSpec(
            num_scalar_prefetch=2,
            grid=(f // tf, nb),
            in_specs=[
                pl.BlockSpec((tb, d), lambda j, b, be, nu: (live(b, nu), 0)),
                pl.BlockSpec((None, None, d, tf), lambda j, b, be, nu: (layer, be[live(b, nu)], 0, j)),
                pl.BlockSpec((None, None, d, tf), lambda j, b, be, nu: (layer, be[live(b, nu)], 0, j)),
            ],
            out_specs=pl.BlockSpec((tb, tf), lambda j, b, be, nu: (b, j)),
        ),
        compiler_params=_params(("parallel", "arbitrary")),
        name="swiglu_up",
    )(blk_e, n_used, xb, w1, w3)


def _down_kernel(be_ref, nu_ref, g_ref, w2_ref, o_ref):
    b = pl.program_id(1)

    @pl.when(b < nu_ref[0])
    def _():
        o_ref[...] = jnp.dot(g_ref[...], w2_ref[...].astype(BF16), preferred_element_type=F32)

    @pl.when(b >= nu_ref[0])
    def _():
        o_ref[...] = jnp.zeros_like(o_ref)


def _down_res_kernel(be_ref, nu_ref, g_ref, w2_ref, x_ref, gt_ref, o_ref, *, n_lat):
    tb = x_ref.shape[0]
    gate = _row_select(gt_ref, pl.program_id(1) * tb, tb, n_lat)
    o_ref[...] = x_ref[...] + gate * jnp.dot(g_ref[...], w2_ref[...].astype(BF16), preferred_element_type=F32)


def _swiglu_down(gact, blk_e, n_used, w2, layer, tb, resid=None):
    rows, f = gact.shape
    d = w2.shape[3]
    wb = w2.dtype.itemsize
    tn = _fit_tile(d, (1024, 512, 256),
                   lambda t: 4 * tb * f + 2 * f * t * wb + 16 * tb * t + 2 * f * t + 4 * tb * t)
    nb = rows // tb
    live = lambda b, nu: jnp.minimum(b, nu[0] - 1)
    in_specs = [
        pl.BlockSpec((tb, f), lambda n, b, be, nu: (live(b, nu), 0)),
        pl.BlockSpec((None, None, f, tn), lambda n, b, be, nu: (layer, be[live(b, nu)], 0, n)),
    ]
    args = [gact, w2]
    if resid is None:
        kern = _down_kernel
    else:
        x, gt, n_lat = resid
        kern = functools.partial(_down_res_kernel, n_lat=n_lat)
        in_specs += [pl.BlockSpec((tb, tn), lambda n, b, be, nu: (b, n)),
                     pl.BlockSpec((2, tn), lambda n, b, be, nu: (0, n))]
        args += [x, gt]
    return pl.pallas_call(
        kern,
        out_shape=jax.ShapeDtypeStruct((rows, d), F32),
        grid_spec=pltpu.PrefetchScalarGridSpec(
            num_scalar_prefetch=2,
            grid=(d // tn, nb),
            in_specs=in_specs,
            out_specs=pl.BlockSpec((tb, tn), lambda n, b, be, nu: (b, n)),
        ),
        compiler_params=_params(("parallel", "arbitrary")),
        name="swiglu_down",
    )(blk_e, n_used, *args)


def _dense_ffn(x, g2, sh, sc, gt, w1, w3, w2, layer, n_lat):
    r = x.shape[0]
    tb = _pick(r, (640, 256))
    nb = r // tb
    blk_e = jnp.zeros((nb,), I32)
    n_used = jnp.full((1,), nb, I32)
    h = _ffn_norm(x, g2, sh, sc, n_lat)
    gact = _swiglu_up(h, blk_e, n_used, w1[:, None], w3[:, None], layer, tb)
    return _swiglu_down(gact, blk_e, n_used, w2[:, None], layer, tb, resid=(x, gt, n_lat))


def _route_kernel(x_ref, g_ref, sh_ref, sc_ref, wr_ref, br_ref, mi_ref, mf_ref, cnt_ref, h_s, run_s, *, n_lat, n_exp):
    i = pl.program_id(0)
    tm = x_ref.shape[0]

    @pl.when(i == 0)
    def _():
        run_s[...] = jnp.zeros_like(run_s)

    _norm_mod_rows(x_ref, g_ref, sh_ref, sc_ref, h_s, i * tm, n_lat)
    logits = jnp.dot(h_s[...], wr_ref[...], preferred_element_type=F32, precision=lax.Precision.HIGHEST) + br_ref[...]
    lane = lax.broadcasted_iota(I32, logits.shape, 1).astype(F32)
    l1 = jnp.where(lane < n_exp, logits, -jnp.inf)
    v1 = l1.max(axis=1, keepdims=True)
    e1 = jnp.where(l1 == v1, lane, float(V7X_LANES)).min(axis=1, keepdims=True)
    l2 = jnp.where(lane == e1, -jnp.inf, l1)
    v2 = l2.max(axis=1, keepdims=True)
    e2 = jnp.where(l2 == v2, lane, float(V7X_LANES)).min(axis=1, keepdims=True)
    t = jnp.exp(v2 - v1)
    g1 = 1.0 / (1.0 + t)
    g2 = t / (1.0 + t)

    onehot = jnp.where(jnp.logical_or(lane == e1, lane == e2), 1.0, 0.0)
    below = lax.broadcasted_iota(I32, (tm, tm), 0) > lax.broadcasted_iota(I32, (tm, tm), 1)
    before = jnp.dot(jnp.where(below, 1.0, 0.0).astype(BF16), onehot.astype(BF16),
                     preferred_element_type=F32) + run_s[...]
    r1 = jnp.where(lane == e1, before, 0.0).sum(axis=1, keepdims=True)
    r2 = jnp.where(lane == e2, before, 0.0).sum(axis=1, keepdims=True)
    run_s[...] = run_s[...] + onehot.sum(axis=0, keepdims=True)

    meta = jnp.where(lane == 0, e1, jnp.where(lane == 1, e2, jnp.where(lane == 2, r1, jnp.where(lane == 3, r2, 0.0))))
    mi_ref[...] = meta.astype(I32)
    mf_ref[...] = jnp.where(lane == 0, g1, jnp.where(lane == 1, g2, 0.0))
    cnt_ref[...] = jnp.broadcast_to(run_s[...], cnt_ref.shape)


def _route(x, g2, sh, sc, wr_pad, br_pad, n_lat, n_exp):
    r, d = x.shape
    tm = ROW_TILE
    return pl.pallas_call(
        functools.partial(_route_kernel, n_lat=n_lat, n_exp=n_exp),
        out_shape=(jax.ShapeDtypeStruct((r, V7X_LANES), I32),
                   jax.ShapeDtypeStruct((r, V7X_LANES), F32),
                   jax.ShapeDtypeStruct((V7X_SUBLANES, V7X_LANES), F32)),
        grid=(r // tm,),
        in_specs=[pl.BlockSpec((tm, d), lambda i: (i, 0)),
                  pl.BlockSpec((1, d), lambda i: (0, 0)),
                  pl.BlockSpec((2, d), lambda i: (0, 0)),
                  pl.BlockSpec((2, d), lambda i: (0, 0)),
                  pl.BlockSpec((d, V7X_LANES), lambda i: (0, 0)),
                  pl.BlockSpec((1, V7X_LANES), lambda i: (0, 0))],
        out_specs=(pl.BlockSpec((tm, V7X_LANES), lambda i: (i, 0)),
                   pl.BlockSpec((tm, V7X_LANES), lambda i: (i, 0)),
                   pl.BlockSpec((V7X_SUBLANES, V7X_LANES), lambda i: (0, 0))),
        scratch_shapes=[pltpu.VMEM((tm, d), F32), pltpu.VMEM((1, V7X_LANES), F32)],
        compiler_params=_params(("arbitrary",)),
        name="route",
    )(x, g2, sh, sc, wr_pad, br_pad)


def _row_copy(src, s_row, dst, d_row, sem):
    return pltpu.make_async_copy(src.at[pl.ds(s_row, 1), :], dst.at[pl.ds(d_row, 1), :], sem)


def _dispatch_kernel(d1_ref, d2_ref, x_ref, g_ref, sh_ref, sc_ref, xb_in_ref, xb_ref, h_s, sem, *, n_lat):
    del xb_in_ref
    i = pl.program_id(0)
    tm = x_ref.shape[0]
    _norm_mod_rows(x_ref, g_ref, sh_ref, sc_ref, h_s, i * tm, n_lat)

    def start(r, c):
        _row_copy(h_s, r, xb_ref, d1_ref[0, r], sem.at[0]).start()
        _row_copy(h_s, r, xb_ref, d2_ref[0, r], sem.at[1]).start()
        return c

    def wait(r, c):
        _row_copy(h_s, r, xb_ref, d1_ref[0, r], sem.at[0]).wait()
        _row_copy(h_s, r, xb_ref, d2_ref[0, r], sem.at[1]).wait()
        return c

    lax.fori_loop(0, tm, start, 0)
    lax.fori_loop(0, tm, wait, 0)


def _dispatch(x, g2, sh, sc, d1, d2, n_rows, n_lat):
    r, d = x.shape
    tm = ROW_TILE
    nt = r // tm
    smem_rows = pl.BlockSpec((None, 1, tm), lambda i: (i, 0, 0), memory_space=pltpu.SMEM)
    return pl.pallas_call(
        functools.partial(_dispatch_kernel, n_lat=n_lat),
        out_shape=jax.ShapeDtypeStruct((n_rows, d), F32),
        grid=(nt,),
        in_specs=[smem_rows, smem_rows,
                  pl.BlockSpec((tm, d), lambda i: (i, 0)),
                  pl.BlockSpec((1, d), lambda i: (0, 0)),
                  pl.BlockSpec((2, d), lambda i: (0, 0)),
                  pl.BlockSpec((2, d), lambda i: (0, 0)),
                  pl.BlockSpec(memory_space=pl.ANY)],
        out_specs=pl.BlockSpec(memory_space=pl.ANY),
        scratch_shapes=[pltpu.VMEM((tm, d), F32), pltpu.SemaphoreType.DMA((2,))],
        input_output_aliases={6: 0},
        compiler_params=_params(("arbitrary",)),
        name="moe_dispatch",
    )(d1.reshape(nt, 1, tm), d2.reshape(nt, 1, tm), x, g2, sh, sc, jnp.zeros((n_rows, d), F32))


def _combine_kernel(d1_ref, d2_ref, mf_ref, x_ref, gt_ref, yb_ref, o_ref, buf, sem, *, n_lat):
    i = pl.program_id(0)
    tm = x_ref.shape[0]

    def start(r, c):
        _row_copy(yb_ref, d1_ref[0, r], buf.at[0], r, sem.at[0]).start()
        _row_copy(yb_ref, d2_ref[0, r], buf.at[1], r, sem.at[1]).start()
        return c

    def wait(r, c):
        _row_copy(yb_ref, d1_ref[0, r], buf.at[0], r, sem.at[0]).wait()
        _row_copy(yb_ref, d2_ref[0, r], buf.at[1], r, sem.at[1]).wait()
        return c

    lax.fori_loop(0, tm, start, 0)
    lax.fori_loop(0, tm, wait, 0)
    f = mf_ref[:, 0:1] * buf[0] + mf_ref[:, 1:2] * buf[1]
    o_ref[...] = x_ref[...] + _row_select(gt_ref, i * tm, tm, n_lat) * f


def _combine(x, gt, yb, d1, d2, mf, n_lat):
    r, d = x.shape
    tm = ROW_TILE
    nt = r // tm
    smem_rows = pl.BlockSpec((None, 1, tm), lambda i: (i, 0, 0), memory_space=pltpu.SMEM)
    return pl.pallas_call(
        functools.partial(_combine_kernel, n_lat=n_lat),
        out_shape=jax.ShapeDtypeStruct((r, d), F32),
        grid=(nt,),
        in_specs=[smem_rows, smem_rows,
                  pl.BlockSpec((tm, V7X_LANES), lambda i: (i, 0)),
                  pl.BlockSpec((tm, d), lambda i: (i, 0)),
                  pl.BlockSpec((2, d), lambda i: (0, 0)),
                  pl.BlockSpec(memory_space=pl.ANY)],
        out_specs=pl.BlockSpec((tm, d), lambda i: (i, 0)),
        scratch_shapes=[pltpu.VMEM((2, tm, d), F32), pltpu.SemaphoreType.DMA((2,))],
        compiler_params=_params(("arbitrary",)),
        name="moe_combine",
    )(d1.reshape(nt, 1, tm), d2.reshape(nt, 1, tm), mf, x, gt, yb)


def _moe_ffn(x, g2, sh, sc, gt, wr, br, w1, w3, w2, layer, n_lat):
    r, d = x.shape
    n_exp = wr.shape[1]
    blk = MOE_BLOCK
    wr_pad = jnp.zeros((d, V7X_LANES), F32).at[:, :n_exp].set(wr)
    br_pad = jnp.zeros((1, V7X_LANES), F32).at[0, :n_exp].set(br)
    mi, mf, cnt = _route(x, g2, sh, sc, wr_pad, br_pad, n_lat, n_exp)

    counts = cnt[0, :n_exp].astype(I32)
    padded = (counts + blk - 1) // blk * blk
    pad_end = jnp.cumsum(padded)
    pad_start = pad_end - padded
    d1 = pad_start[mi[:, 0]] + mi[:, 2]
    d2 = pad_start[mi[:, 1]] + mi[:, 3]
    n_rows = -(-(TOP_K * r + n_exp * (blk - 1)) // blk) * blk
    nb = n_rows // blk
    blk_e = jnp.minimum(jnp.searchsorted(pad_end, jnp.arange(nb, dtype=I32) * blk, side="right"),
                        n_exp - 1).astype(I32)
    n_used = (pad_end[-1:] // blk).astype(I32)

    xb = _dispatch(x, g2, sh, sc, d1, d2, n_rows, n_lat)
    gact = _swiglu_up(xb, blk_e, n_used, w1, w3, layer, blk)
    yb = _swiglu_down(gact, blk_e, n_used, w2, layer, blk)
    return _combine(x, gt, yb, d1, d2, mf, n_lat)


def _rope_tables(n_lat, n_ctx):
    n_rows = n_lat // GRID_W
    row = jnp.repeat(jnp.arange(n_rows, dtype=F32), GRID_W)
    col = jnp.tile(jnp.arange(GRID_W, dtype=F32), n_rows)
    axis_dim = HEAD_DIM // 2
    inv_freq = ROPE_THETA ** (-jnp.arange(0, axis_dim, 2, dtype=F32) / axis_dim)
    ang_r = row[:, None] * inv_freq[None, :]
    ang_c = col[:, None] * inv_freq[None, :]
    cr, sr, cc, sc = jnp.cos(ang_r), jnp.sin(ang_r), jnp.cos(ang_c), jnp.sin(ang_c)
    cos_t = jnp.concatenate([cr, cr, cc, cc], axis=1)
    sin_t = jnp.concatenate([-sr, sr, -sc, sc], axis=1)
    cos_t = jnp.concatenate([cos_t, jnp.ones((n_ctx, HEAD_DIM), F32)], axis=0)
    sin_t = jnp.concatenate([sin_t, jnp.zeros((n_ctx, HEAD_DIM), F32)], axis=0)
    return cos_t, sin_t


def kernel(x, c, ctx, c_ctx, w_ada, b_ada, g_norm1, w_in, conv_w, conv_b, conv_ln_g, conv_ln_b, conv_pw, conv_pw_b, sgu_ln_g, sgu_ln_b, sgu_w, sgu_b, swa_q_g, swa_k_g, swa_sink, glb_q_g, glb_k_g, g_branch, w_out, g_norm2, ffn_w1, ffn_w3, ffn_w2, router_w, router_b, exp_w1, exp_w3, exp_w2):
    batch, n_lat, d = x.shape
    n_ctx = ctx.shape[1]
    depth = w_ada.shape[0]
    gw = d // N_GROUPS
    assert batch == 1 and n_lat % n_ctx == 0 and n_ctx % ROW_TILE == 0 and n_lat % GRID_W == 0
    assert conv_w.shape[1] // 2 < CONV_HALO and w_in.shape[2] == 4 * 2 * gw

    xs = jnp.concatenate([x[0], ctx[0]], axis=0)
    mods = _ada(jnp.stack([c[0], c_ctx], axis=1), w_ada, b_ada)
    cos_t, sin_t = _rope_tables(n_lat, n_ctx)
    row2 = lambda v: v.reshape(1, -1)

    for l in range(depth):
        sh1, sc1, gt1, sh2, sc2, gt2 = (mods[l, :, k * d:(k + 1) * d] for k in range(6))

        p = _inproj(xs, row2(g_norm1[l]), sh1, sc1, w_in[l].astype(BF16), n_lat)
        y_conv = _conv_group(p, conv_w[l], row2(conv_b[l]), row2(conv_ln_g[l]), row2(conv_ln_b[l]),
                             conv_pw[l].astype(BF16), row2(conv_pw_b[l]), n_lat)
        y_sgu = _sgu_group(p, row2(sgu_ln_g[l]), row2(sgu_ln_b[l]), sgu_w[l].astype(BF16), sgu_b[l].T)
        q, k, v, qt, vt = _prep(p, jnp.stack([swa_q_g[l], glb_q_g[l]])[:, None, :],
                        jnp.stack([swa_k_g[l], glb_k_g[l]])[:, None, :], cos_t, sin_t, first_block=2)
        y_swa = _swa(q, k, v, swa_sink[l], n_lat, grp=0)
        y_glb = _glb(qt, k, vt, n_lat, grp=1)
        xs = _outproj((y_conv, y_sgu, y_swa, y_glb), row2(g_branch[l]), w_out[l].astype(BF16), xs, gt1, n_lat)

        if l % 2 == 0:
            xs = _dense_ffn(xs, row2(g_norm2[l]), sh2, sc2, gt2, ffn_w1, ffn_w3, ffn_w2, l // 2, n_lat)
        else:
            xs = _moe_ffn(xs, row2(g_norm2[l]), sh2, sc2, gt2, router_w[l // 2], router_b[l // 2],
                          exp_w1, exp_w3, exp_w2, l // 2, n_lat)
    return xs[:n_lat][None]
```

```python
import functools
import math

import jax
import jax.numpy as jnp
from jax import lax
from jax.experimental import pallas as pl
from jax.experimental.pallas import tpu as pltpu

F32 = jnp.float32
BF16 = jnp.bfloat16
I32 = jnp.int32

HEAD_DIM = 128
N_GROUPS = 4
N_KV_HEADS = 2
GRID_W = 64
WINDOW = 128
ROPE_THETA = 10000.0
TOP_K = 2
MOE_BLOCK = 512
EPS = 1e-6
NEG = -1e30
SCALE = HEAD_DIM ** -0.5
Q_PRESCALE = SCALE * math.log2(math.e)
EXP2_SAFE_SHIFT = 60.0
KEY_NORM_MARGIN = 1.01

V7X_VMEM_BYTES = 64 * 1024 * 1024
V7X_LANES = 128
V7X_SUBLANES = 8
VMEM_BUDGET = V7X_VMEM_BYTES - 8 * 1024 * 1024

ROW_TILE = 256
CONV_HALO = 16
NORM_CHUNK = 16
NORM_UNROLL = 8
ROW_DMA_UNROLL = 8


def _pick(n, cands):
    for c in cands:
        if n % c == 0:
            return c
    raise ValueError(f"no tile in {cands} divides {n}")


def _params(sem, vmem=VMEM_BUDGET):
    return pltpu.CompilerParams(dimension_semantics=sem, vmem_limit_bytes=vmem)


def _sigmoid(x):
    return 1.0 / (1.0 + jnp.exp(-x))


def _silu(x):
    return x * _sigmoid(x)


def _gelu_tanh(x):
    c = math.sqrt(2.0 / math.pi)
    return 0.5 * x * (1.0 + jnp.tanh(c * (x + 0.044715 * (x * x * x))))


def _rms(x):
    return x * lax.rsqrt(jnp.mean(x * x, axis=-1, keepdims=True) + EPS)


def _layer_norm(x, g, b):
    xc = x - jnp.mean(x, axis=-1, keepdims=True)
    return xc * lax.rsqrt(jnp.mean(xc * xc, axis=-1, keepdims=True) + EPS) * g + b


def _row_select(vec2_ref, row0, tm, n_lat):
    rows = row0 + lax.broadcasted_iota(I32, (tm, 1), 0)
    return jnp.where(rows >= n_lat, vec2_ref[1:2, :], vec2_ref[0:1, :])


def _norm_mod(x, g, shift, scale):
    return _rms(x) * g * (1.0 + scale) + shift


def _norm_scratch(d):
    return [pltpu.VMEM((2 * V7X_SUBLANES, d), F32), pltpu.VMEM((2 * V7X_SUBLANES, d), F32)]


def _norm_mod_rows(x_ref, g_ref, sh_ref, sc_ref, out_ref, gs_s, sh_s, row0, n_lat):
    tm, d = x_ref.shape
    sub = V7X_SUBLANES
    for t in range(2):
        gs_s[t * sub:(t + 1) * sub, :] = jnp.broadcast_to(g_ref[...] * (1.0 + sc_ref[t:t + 1, :]), (sub, d))
        sh_s[t * sub:(t + 1) * sub, :] = jnp.broadcast_to(sh_ref[t:t + 1, :], (sub, d))

    def body(c, carry):
        r0 = pl.multiple_of(c * NORM_CHUNK, NORM_CHUNK)
        t0 = pl.multiple_of(jnp.where(row0 + r0 >= n_lat, sub, 0), sub)
        x = x_ref[pl.ds(r0, NORM_CHUNK), :].reshape(NORM_CHUNK // sub, sub, d)
        y = _rms(x) * gs_s[pl.ds(t0, sub), :] + sh_s[pl.ds(t0, sub), :]
        out_ref[pl.ds(r0, NORM_CHUNK), :] = y.reshape(NORM_CHUNK, d).astype(out_ref.dtype)
        return carry

    lax.fori_loop(0, tm // NORM_CHUNK, body, 0, unroll=NORM_UNROLL)


def _ada_kernel(s_ref, w_ref, b_ref, o_ref):
    d, tn = w_ref.shape
    kc = 64

    def body(c, acc):
        a0, a1 = acc
        k0 = pl.multiple_of(c * kc, kc)
        w = w_ref[pl.ds(k0, kc), :]
        s = _silu(s_ref[pl.ds(k0, kc), :])
        a0 = a0 + (w * s[:, 0:1]).reshape(kc // V7X_SUBLANES, V7X_SUBLANES, tn).sum(axis=0)
        a1 = a1 + (w * s[:, 1:2]).reshape(kc // V7X_SUBLANES, V7X_SUBLANES, tn).sum(axis=0)
        return a0, a1

    z = jnp.zeros((V7X_SUBLANES, tn), F32)
    a0, a1 = lax.fori_loop(0, d // kc, body, (z, z))
    o_ref[0:1, :] = a0.sum(axis=0, keepdims=True) + b_ref[...]
    o_ref[1:2, :] = a1.sum(axis=0, keepdims=True) + b_ref[...]


def _ada(cond, w_ada, b_ada):
    depth, d, n = w_ada.shape
    tn = _pick(n, (1024, 512, 256, 128))
    return pl.pallas_call(
        _ada_kernel,
        out_shape=jax.ShapeDtypeStruct((depth, 2, n), F32),
        grid=(depth, n // tn),
        in_specs=[
            pl.BlockSpec((d, 2), lambda l, j: (0, 0)),
            pl.BlockSpec((None, d, tn), lambda l, j: (l, 0, j)),
            pl.BlockSpec((None, 1, tn), lambda l, j: (l, 0, j)),
        ],
        out_specs=pl.BlockSpec((None, 2, tn), lambda l, j: (l, 0, j)),
        compiler_params=_params(("parallel", "parallel")),
        name="ada",
    )(cond, w_ada, b_ada.reshape(depth, 1, n))


def _inproj_kernel(x_ref, g_ref, sh_ref, sc_ref, w_ref, o_ref, h_s, gs_s, sh_s, *, n_lat):
    i = pl.program_id(0)
    tm = x_ref.shape[0]

    @pl.when(pl.program_id(1) == 0)
    def _():
        _norm_mod_rows(x_ref, g_ref, sh_ref, sc_ref, h_s, gs_s, sh_s, i * tm, n_lat)

    o_ref[...] = jnp.dot(h_s[...], w_ref[...], preferred_element_type=F32)


def _inproj(x, g, sh, sc, w, n_lat):
    r, d = x.shape
    n = w.shape[1]
    tm = _pick(r, (640, 256))
    tn = _pick(n, (1024, 512, 256))
    return pl.pallas_call(
        functools.partial(_inproj_kernel, n_lat=n_lat),
        out_shape=jax.ShapeDtypeStruct((r, n), F32),
        grid=(r // tm, n // tn),
        in_specs=[
            pl.BlockSpec((tm, d), lambda i, j: (i, 0)),
            pl.BlockSpec((1, d), lambda i, j: (0, 0)),
            pl.BlockSpec((2, d), lambda i, j: (0, 0)),
            pl.BlockSpec((2, d), lambda i, j: (0, 0)),
            pl.BlockSpec((d, tn), lambda i, j: (0, j)),
        ],
        out_specs=pl.BlockSpec((tm, tn), lambda i, j: (i, j)),
        scratch_shapes=[pltpu.VMEM((tm, d), BF16)] + _norm_scratch(d),
        compiler_params=_params(("parallel", "arbitrary")),
        name="inproj",
    )(x, g, sh, sc, w)


def _conv_kernel(pm_ref, pp_ref, pn_ref, cw_ref, cb_ref, lg_ref, lb_ref, pw_ref, pb_ref, o_ref, ext_s, sft_s,
                 *, n_lat_tiles, n_tiles):
    i = pl.program_id(0)
    tm, gw = o_ref.shape
    kw = cw_ref.shape[0]

    def glu(p):
        return p[:, :gw] * _sigmoid(p[:, gw:])

    prev_ok = jnp.logical_and(i != 0, i != n_lat_tiles)
    next_ok = jnp.logical_and(i != n_lat_tiles - 1, i != n_tiles - 1)
    ext_s[0:CONV_HALO, :] = jnp.where(prev_ok, glu(pp_ref[...]), 0.0)
    ext_s[CONV_HALO:CONV_HALO + tm, :] = glu(pm_ref[...])
    ext_s[CONV_HALO + tm:, :] = jnp.where(next_ok, glu(pn_ref[...]), 0.0)

    sub = V7X_SUBLANES
    span = sft_s.shape[1]
    for b in range(1, sub):
        sft_s[b - 1] = ext_s[b:b + span, :]
    acc = jnp.zeros((tm, gw), F32) + cb_ref[...]
    for k in range(kw):
        a, b = divmod(CONV_HALO - kw // 2 + k, sub)
        tap = ext_s[a * sub:a * sub + tm, :] if b == 0 else sft_s[b - 1, a * sub:a * sub + tm, :]
        acc = acc + cw_ref[k:k + 1, :] * tap
    y = _silu(_layer_norm(acc, lg_ref[...], lb_ref[...]))
    o_ref[...] = jnp.dot(y.astype(BF16), pw_ref[...], preferred_element_type=F32) + pb_ref[...]


def _conv_group(p, cw, cb, lg, lb, pw, pb, n_lat):
    r = p.shape[0]
    kw, gw = cw.shape
    tm = ROW_TILE
    n_tiles = r // tm
    hb = tm // CONV_HALO
    last_hb = r // CONV_HALO - 1
    return pl.pallas_call(
        functools.partial(_conv_kernel, n_lat_tiles=n_lat // tm, n_tiles=n_tiles),
        out_shape=jax.ShapeDtypeStruct((r, gw), F32),
        grid=(n_tiles,),
        in_specs=[
            pl.BlockSpec((tm, 2 * gw), lambda i: (i, 0)),
            pl.BlockSpec((CONV_HALO, 2 * gw), lambda i: (jnp.maximum(i * hb - 1, 0), 0)),
            pl.BlockSpec((CONV_HALO, 2 * gw), lambda i: (jnp.minimum((i + 1) * hb, last_hb), 0)),
            pl.BlockSpec((kw, gw), lambda i: (0, 0)),
            pl.BlockSpec((1, gw), lambda i: (0, 0)),
            pl.BlockSpec((1, gw), lambda i: (0, 0)),
            pl.BlockSpec((1, gw), lambda i: (0, 0)),
            pl.BlockSpec((gw, gw), lambda i: (0, 0)),
            pl.BlockSpec((1, gw), lambda i: (0, 0)),
        ],
        out_specs=pl.BlockSpec((tm, gw), lambda i: (i, 0)),
        scratch_shapes=[pltpu.VMEM((tm + 2 * CONV_HALO, gw), F32),
                        pltpu.VMEM((V7X_SUBLANES - 1, tm + 2 * CONV_HALO - V7X_SUBLANES, gw), F32)],
        compiler_params=_params(("parallel",)),
        name="conv_group",
    )(p, p, p, cw, cb, lg, lb, pw, pb)


def _sgu_kernel(p_ref, lg_ref, lb_ref, ws_ref, bs_ref, o_ref):
    tm, gw = o_ref.shape
    n_h, ch, _ = ws_ref.shape
    hd = gw // n_h
    z = _gelu_tanh(p_ref[...])
    u = z[:, :gw]
    v = _layer_norm(z[:, gw:], lg_ref[...], lb_ref[...]).astype(BF16)
    for c in range(tm // ch):
        rows = slice(c * ch, (c + 1) * ch)
        parts = []
        for h in range(n_h):
            s = jnp.dot(ws_ref[h], v[rows, h * hd:(h + 1) * hd], preferred_element_type=F32)
            parts.append(s + bs_ref[:, h:h + 1])
        o_ref[rows, :] = u[rows, :] * jnp.concatenate(parts, axis=1)


def _sgu_group(p, lg, lb, ws, bs_t):
    r = p.shape[0]
    gw = lg.shape[1]
    n_h, ch, _ = ws.shape
    tm = ROW_TILE
    return pl.pallas_call(
        _sgu_kernel,
        out_shape=jax.ShapeDtypeStruct((r, gw), F32),
        grid=(r // tm,),
        in_specs=[
            pl.BlockSpec((tm, 2 * gw), lambda i: (i, 1)),
            pl.BlockSpec((1, gw), lambda i: (0, 0)),
            pl.BlockSpec((1, gw), lambda i: (0, 0)),
            pl.BlockSpec((n_h, ch, ch), lambda i: (0, 0, 0)),
            pl.BlockSpec((ch, n_h), lambda i: (0, 0)),
        ],
        out_specs=pl.BlockSpec((tm, gw), lambda i: (i, 0)),
        compiler_params=_params(("parallel",)),
        name="sgu_group",
    )(p, lg, lb, ws, bs_t)


def _prep_kernel(p_ref, qg_ref, kg_ref, cos_ref, sin_ref, q_ref, k_ref, v_ref, qt_ref, vt_ref, kn_ref):
    tm = p_ref.shape[0]
    qw, kvw = q_ref.shape[1], k_ref.shape[1]
    cos = cos_ref[...]
    sin = sin_ref[...]
    lane = lax.broadcasted_iota(I32, (tm, HEAD_DIM), 1)
    low = (lane & (HEAD_DIM // 4)) == 0

    def head(x, g, mult):
        y = _rms(x) * g
        partner = jnp.where(low, pltpu.roll(y, HEAD_DIM - HEAD_DIM // 4, 1), pltpu.roll(y, HEAD_DIM // 4, 1))
        return (y * cos + partner * sin) * mult

    for h in range(qw // HEAD_DIM):
        cols = slice(h * HEAD_DIM, (h + 1) * HEAD_DIM)
        qh = head(p_ref[:, cols], qg_ref[...], Q_PRESCALE)
        q_ref[:, cols] = qh.astype(BF16)
        qt_ref[cols, :] = qh.T.astype(BF16)
    kn_lane = lax.broadcasted_iota(I32, kn_ref.shape, 1)
    kn = jnp.zeros(kn_ref.shape, F32)
    for h in range(kvw // HEAD_DIM):
        cols = slice(h * HEAD_DIM, (h + 1) * HEAD_DIM)
        kh = head(p_ref[:, qw + h * HEAD_DIM:qw + (h + 1) * HEAD_DIM], kg_ref[...], 1.0)
        k_ref[:, cols] = kh.astype(BF16)
        ksq = jnp.sum(kh * kh, axis=1, keepdims=True).max(axis=0, keepdims=True)
        kn = jnp.where(kn_lane == h, ksq, kn)
        vh = p_ref[:, qw + kvw + h * HEAD_DIM:qw + kvw + (h + 1) * HEAD_DIM]
        v_ref[:, cols] = vh.astype(BF16)
        vt_ref[cols, :] = vh.T.astype(BF16)
    kn_ref[...] = kn


def _prep(p, qg, kg, cos_t, sin_t, first_block):
    r = p.shape[0]
    n_att = qg.shape[0]
    qw = N_GROUPS * HEAD_DIM
    kvw = N_KV_HEADS * HEAD_DIM
    tm = ROW_TILE
    rows = lambda w: pl.BlockSpec((None, tm, w), lambda i, a: (a, i, 0))
    cols = lambda w: pl.BlockSpec((None, w, tm), lambda i, a: (a, 0, i))
    return pl.pallas_call(
        _prep_kernel,
        out_shape=(jax.ShapeDtypeStruct((n_att, r, qw), BF16),
                   jax.ShapeDtypeStruct((n_att, r, kvw), BF16),
                   jax.ShapeDtypeStruct((n_att, r, kvw), BF16),
                   jax.ShapeDtypeStruct((n_att, qw, r), BF16),
                   jax.ShapeDtypeStruct((n_att, kvw, r), BF16),
                   jax.ShapeDtypeStruct((n_att, r // tm, V7X_SUBLANES, V7X_LANES), F32)),
        grid=(r // tm, n_att),
        in_specs=[
            pl.BlockSpec((tm, qw + 2 * kvw), lambda i, a: (i, first_block + a)),
            pl.BlockSpec((None, 1, HEAD_DIM), lambda i, a: (a, 0, 0)),
            pl.BlockSpec((None, 1, HEAD_DIM), lambda i, a: (a, 0, 0)),
            pl.BlockSpec((tm, HEAD_DIM), lambda i, a: (i, 0)),
            pl.BlockSpec((tm, HEAD_DIM), lambda i, a: (i, 0)),
        ],
        out_specs=(rows(qw), rows(kvw), rows(kvw), cols(qw), cols(kvw),
                   pl.BlockSpec((None, None, V7X_SUBLANES, V7X_LANES), lambda i, a: (a, i, 0, 0))),
        compiler_params=_params(("parallel", "parallel")),
        name="qkv_prep",
    )(p, qg, kg, cos_t, sin_t)


def _nt_dot(a, b):
    return lax.dot_general(a, b, (((1,), (1,)), ((), ())), preferred_element_type=F32)


def _swa_kernel(sink_ref, q_ref, km_ref, kp_ref, kn_ref, vm_ref, vp_ref, vn_ref, kc_ref, vc_ref, o_ref,
                *, n_lat):
    i = pl.program_id(0)
    tq = q_ref.shape[0]
    hb = kp_ref.shape[0]
    n_rep = (q_ref.shape[1] // HEAD_DIM) // N_KV_HEADS
    nk = tq + 2 * hb
    qi = lax.broadcasted_iota(I32, (tq, nk), 0)
    kj = lax.broadcasted_iota(I32, (tq, nk), 1) - hb
    kglob = i * tq + kj
    ok1 = (jnp.abs(kj - qi) <= WINDOW) & (kglob >= 0) & (kglob < n_lat) & (i * tq < n_lat)
    ok = jnp.concatenate([ok1.astype(F32)] * n_rep, axis=0) > 0.5
    rep_of_row = jnp.concatenate([jnp.full((tq, 1), g, I32) for g in range(n_rep)], axis=0)
    for j in range(N_KV_HEADS):
        kv = slice(j * HEAD_DIM, (j + 1) * HEAD_DIM)
        q2 = jnp.concatenate(
            [q_ref[:, (j * n_rep + g) * HEAD_DIM:(j * n_rep + g + 1) * HEAD_DIM] for g in range(n_rep)], axis=0)
        kw = jnp.concatenate([kp_ref[:, kv], km_ref[:, kv], kn_ref[:, kv]], axis=0)
        vw = jnp.concatenate([vp_ref[:, kv], vm_ref[:, kv], vn_ref[:, kv]], axis=0)
        s_w = jnp.where(ok, _nt_dot(q2, kw), NEG)
        s_c = _nt_dot(q2, kc_ref[:, kv])
        sk = jnp.zeros((n_rep * tq, 1), F32)
        for g in range(n_rep):
            sk = jnp.where(rep_of_row == g, sink_ref[j * n_rep + g] * math.log2(math.e), sk)
        m = jnp.maximum(jnp.maximum(s_w.max(axis=1, keepdims=True), s_c.max(axis=1, keepdims=True)), sk)
        p_w = jnp.exp2(s_w - m)
        p_c = jnp.exp2(s_c - m)
        den = jnp.exp2(sk - m) + p_w.sum(axis=1, keepdims=True) + p_c.sum(axis=1, keepdims=True)
        o = (jnp.dot(p_w.astype(BF16), vw, preferred_element_type=F32)
             + jnp.dot(p_c.astype(BF16), vc_ref[:, kv], preferred_element_type=F32)) / den
        for g in range(n_rep):
            h = j * n_rep + g
            o_ref[:, h * HEAD_DIM:(h + 1) * HEAD_DIM] = o[g * tq:(g + 1) * tq, :]


def _swa(q, k, v, sink, n_lat, grp):
    _, r, qw = q.shape
    kvw = k.shape[2]
    n_ctx = r - n_lat
    tq = ROW_TILE
    hb = WINDOW
    per = tq // hb
    last_hb = r // hb - 1
    main = lambda i: (grp, i, 0)
    prev = lambda i: (grp, jnp.maximum(i * per - 1, 0), 0)
    nxt = lambda i: (grp, jnp.minimum((i + 1) * per, last_hb), 0)
    ctx = lambda i: (grp, n_lat // n_ctx, 0)
    return pl.pallas_call(
        functools.partial(_swa_kernel, n_lat=n_lat),
        out_shape=jax.ShapeDtypeStruct((r, qw), F32),
        grid=(r // tq,),
        in_specs=[
            pl.BlockSpec(memory_space=pltpu.SMEM),
            pl.BlockSpec((None, tq, qw), main),
            pl.BlockSpec((None, tq, kvw), main),
            pl.BlockSpec((None, hb, kvw), prev),
            pl.BlockSpec((None, hb, kvw), nxt),
            pl.BlockSpec((None, tq, kvw), main),
            pl.BlockSpec((None, hb, kvw), prev),
            pl.BlockSpec((None, hb, kvw), nxt),
            pl.BlockSpec((None, n_ctx, kvw), ctx),
            pl.BlockSpec((None, n_ctx, kvw), ctx),
        ],
        out_specs=pl.BlockSpec((tq, qw), lambda i: (i, 0)),
        compiler_params=_params(("parallel",)),
        name="window_attn",
    )(sink, q, k, k, k, v, v, v, k, v)


def _glb_kernel(kmax_ref, qt_ref, k_ref, vt_ref, o_ref, acc_s, l_s, *, n_lat, tk, unroll):
    j = pl.program_id(0)
    i = pl.program_id(1)
    tq = qt_ref.shape[1]
    n_rep = qt_ref.shape[0] // HEAD_DIM
    nq = n_rep * tq
    n_ctx = k_ref.shape[0] - n_lat
    q2t = jnp.concatenate([qt_ref[g * HEAD_DIM:(g + 1) * HEAD_DIM, :] for g in range(n_rep)], axis=1)
    qf = q2t.astype(F32)
    bound = jnp.sqrt(jnp.sum(qf * qf, axis=0, keepdims=True)) * kmax_ref[j]

    def scores(k0, size):
        return jnp.dot(k_ref[pl.ds(k0, size), :], q2t, preferred_element_type=F32)

    def shifted_by_bound():
        def add(k0, size):
            p = jnp.exp2(scores(k0, size) - bound)
            l_s[...] += p.sum(axis=0, keepdims=True)
            acc_s[...] += jnp.dot(vt_ref[:, pl.ds(k0, size)], p.astype(BF16), preferred_element_type=F32)

        acc_s[...] = jnp.zeros_like(acc_s)
        l_s[...] = jnp.zeros_like(l_s)
        add(n_lat, n_ctx)

        @pl.when(i * tq < n_lat)
        def _():
            def body(c, carry):
                for u in range(unroll):
                    add(pl.multiple_of((c * unroll + u) * tk, tk), tk)
                return carry

            lax.fori_loop(0, n_lat // tk // unroll, body, 0)

    def shifted_by_running_max():
        def update(s, vt, m, l):
            m_new = jnp.maximum(m, s.max(axis=0, keepdims=True))
            alpha = jnp.exp2(m - m_new)
            p = jnp.exp2(s - m_new)
            l = alpha * l + p.sum(axis=0, keepdims=True)
            acc_s[...] = alpha * acc_s[...] + jnp.dot(vt, p.astype(BF16), preferred_element_type=F32)
            return m_new, l

        acc_s[...] = jnp.zeros_like(acc_s)
        first = update(scores(n_lat, n_ctx), vt_ref[:, n_lat:],
                       jnp.full((1, nq), NEG, F32), jnp.zeros((1, nq), F32))

        def body(c, carry):
            k0 = pl.multiple_of(c * tk, tk)
            return update(scores(k0, tk), vt_ref[:, pl.ds(k0, tk)], *carry)

        _, l = lax.fori_loop(0, jnp.where(i * tq < n_lat, n_lat // tk, 0), body, first)
        l_s[...] = l

    lax.cond(jnp.max(bound) <= EXP2_SAFE_SHIFT, shifted_by_bound, shifted_by_running_max)
    o = (acc_s[...] / l_s[...]).T
    for g in range(n_rep):
        o_ref[:, g * HEAD_DIM:(g + 1) * HEAD_DIM] = o[g * tq:(g + 1) * tq, :]


def _glb(qt, k, vt, kmax, n_lat, grp):
    _, r, kvw = k.shape
    qw = qt.shape[1]
    tq = ROW_TILE
    gq = qw // N_KV_HEADS
    tk = _pick(n_lat, (2048, 1024, 512, 256))
    unroll = 2 if (n_lat // tk) % 2 == 0 else 1
    nq = gq // HEAD_DIM * tq
    return pl.pallas_call(
        functools.partial(_glb_kernel, n_lat=n_lat, tk=tk, unroll=unroll),
        out_shape=jax.ShapeDtypeStruct((r, qw), F32),
        grid=(N_KV_HEADS, r // tq),
        in_specs=[
            pl.BlockSpec(memory_space=pltpu.SMEM),
            pl.BlockSpec((None, gq, tq), lambda j, i: (grp, j, i)),
            pl.BlockSpec((None, r, HEAD_DIM), lambda j, i: (grp, 0, j)),
            pl.BlockSpec((None, HEAD_DIM, r), lambda j, i: (grp, j, 0)),
        ],
        out_specs=pl.BlockSpec((tq, gq), lambda j, i: (i, j)),
        scratch_shapes=[pltpu.VMEM((HEAD_DIM, nq), F32), pltpu.VMEM((1, nq), F32)],
        compiler_params=_params(("parallel", "parallel")),
        name="global_attn",
    )(kmax, qt, k, vt)


def _outproj_kernel(y0_ref, y1_ref, y2_ref, y3_ref, gb_ref, w_ref, x_ref, gt_ref, o_ref, yn_s, *, n_lat):
    i = pl.program_id(0)
    tm = x_ref.shape[0]

    @pl.when(pl.program_id(1) == 0)
    def _():
        for g, y_ref in enumerate((y0_ref, y1_ref, y2_ref, y3_ref)):
            gw = y_ref.shape[1]
            cols = slice(g * gw, (g + 1) * gw)
            yn_s[:, cols] = (_rms(y_ref[...]) * gb_ref[:, cols]).astype(BF16)

    gate = _row_select(gt_ref, i * tm, tm, n_lat)
    o_ref[...] = x_ref[...] + gate * jnp.dot(yn_s[...], w_ref[...], preferred_element_type=F32)


def _outproj(ys, gb, w, x, gt, n_lat):
    r, d = x.shape
    gw = ys[0].shape[1]
    tm = _pick(r, (640, 256))
    tn = _pick(d, (1024, 512, 256))
    ysp = pl.BlockSpec((tm, gw), lambda i, j: (i, 0))
    return pl.pallas_call(
        functools.partial(_outproj_kernel, n_lat=n_lat),
        out_shape=jax.ShapeDtypeStruct((r, d), F32),
        grid=(r // tm, d // tn),
        in_specs=[ysp, ysp, ysp, ysp,
                  pl.BlockSpec((1, d), lambda i, j: (0, 0)),
                  pl.BlockSpec((d, tn), lambda i, j: (0, j)),
                  pl.BlockSpec((tm, tn), lambda i, j: (i, j)),
                  pl.BlockSpec((2, tn), lambda i, j: (0, j))],
        out_specs=pl.BlockSpec((tm, tn), lambda i, j: (i, j)),
        scratch_shapes=[pltpu.VMEM((tm, d), BF16)],
        compiler_params=_params(("parallel", "arbitrary")),
        name="outproj",
    )(*ys, gb, w, x, gt)


def _ffn_norm_kernel(x_ref, g_ref, sh_ref, sc_ref, o_ref, gs_s, sh_s, *, n_lat):
    _norm_mod_rows(x_ref, g_ref, sh_ref, sc_ref, o_ref, gs_s, sh_s, pl.program_id(0) * x_ref.shape[0], n_lat)


def _ffn_norm(x, g, sh, sc, n_lat):
    r, d = x.shape
    tm = _pick(r, (640, 256))
    return pl.pallas_call(
        functools.partial(_ffn_norm_kernel, n_lat=n_lat),
        out_shape=jax.ShapeDtypeStruct((r, d), BF16),
        grid=(r // tm,),
        in_specs=[pl.BlockSpec((tm, d), lambda i: (i, 0)),
                  pl.BlockSpec((1, d), lambda i: (0, 0)),
                  pl.BlockSpec((2, d), lambda i: (0, 0)),
                  pl.BlockSpec((2, d), lambda i: (0, 0))],
        out_specs=pl.BlockSpec((tm, d), lambda i: (i, 0)),
        scratch_shapes=_norm_scratch(d),
        compiler_params=_params(("parallel",)),
        name="ffn_norm",
    )(x, g, sh, sc)


U32 = jnp.uint32
BF16_HI_MASK = 0xFFFF0000


def _pack_bf16_pairs(lo, hi):
    lo_bits = pltpu.bitcast(lo.astype(BF16).astype(F32), U32)
    hi_bits = pltpu.bitcast(hi.astype(BF16).astype(F32), U32)
    return hi_bits | (lo_bits >> 16)


def _unpack_bf16_pairs(packed):
    lo = pltpu.bitcast(packed << 16, F32).astype(BF16)
    hi = pltpu.bitcast(packed & U32(BF16_HI_MASK), F32).astype(BF16)
    return lo, hi


def _up_kernel(be_ref, nu_ref, x_ref, w1_ref, w3_ref, o_ref):
    b = pl.program_id(1)

    @pl.when(b < nu_ref[0])
    def _():
        if x_ref.dtype == U32:
            x = jnp.concatenate(_unpack_bf16_pairs(x_ref[...]), axis=1)
        else:
            x = x_ref[...]
        a = jnp.dot(x, w1_ref[...].astype(BF16), preferred_element_type=F32)
        c = jnp.dot(x, w3_ref[...].astype(BF16), preferred_element_type=F32)
        o_ref[...] = (_silu(a) * c).astype(o_ref.dtype)

    @pl.when(b >= nu_ref[0])
    def _():
        o_ref[...] = jnp.zeros_like(o_ref)


def _fit_tile(n, cands, vmem_bytes):
    for c in cands:
        if n % c == 0 and vmem_bytes(c) <= VMEM_BUDGET:
            return c
    raise ValueError(f"no tile in {cands} divides {n} within the VMEM budget")


def _swiglu_up(xb, blk_e, n_used, w1, w3, layer, tb):
    rows, xw = xb.shape
    d, f = w1.shape[2:]
    wb, xbytes = w1.dtype.itemsize, xb.dtype.itemsize
    tf = _fit_tile(f, (1408, 1024, 512, 256, 128),
                   lambda t: 2 * tb * xw * xbytes + 4 * d * t * wb + 4 * tb * t + 2 * tb * d + 4 * d * t + 16 * tb * t)
    nb = rows // tb
    live = lambda b, nu: jnp.maximum(jnp.minimum(b, nu[0] - 1), 0)
    return pl.pallas_call(
        _up_kernel,
        out_shape=jax.ShapeDtypeStruct((rows, f), BF16),
        grid_spec=pltpu.PrefetchScalarGridSpec(
            num_scalar_prefetch=2,
            grid=(f // tf, nb),
            in_specs=[
                pl.BlockSpec((tb, xw), lambda j, b, be, nu: (live(b, nu), 0)),
                pl.BlockSpec((None, None, d, tf), lambda j, b, be, nu: (layer, be[live(b, nu)], 0, j)),
                pl.BlockSpec((None, None, d, tf), lambda j, b, be, nu: (layer, be[live(b, nu)], 0, j)),
            ],
            out_specs=pl.BlockSpec((tb, tf), lambda j, b, be, nu: (b, j)),
        ),
        compiler_params=_params(("parallel", "arbitrary")),
        name="swiglu_up",
    )(blk_e, n_used, xb, w1, w3)


def _down_kernel(be_ref, nu_ref, g_ref, w2_ref, o_ref):
    b = pl.program_id(1)

    @pl.when(b < nu_ref[0])
    def _():
        o_ref[...] = jnp.dot(g_ref[...], w2_ref[...].astype(BF16), preferred_element_type=F32)

    @pl.when(b >= nu_ref[0])
    def _():
        o_ref[...] = jnp.zeros_like(o_ref)


def _down_res_kernel(be_ref, nu_ref, g_ref, w2_ref, x_ref, gt_ref, o_ref, *, n_lat):
    tb = x_ref.shape[0]
    gate = _row_select(gt_ref, pl.program_id(1) * tb, tb, n_lat)
    o_ref[...] = x_ref[...] + gate * jnp.dot(g_ref[...], w2_ref[...].astype(BF16), preferred_element_type=F32)


def _swiglu_down(gact, blk_e, n_used, w2, layer, tb, resid=None):
    rows, f = gact.shape
    d = w2.shape[3]
    wb = w2.dtype.itemsize
    tn = _fit_tile(d, (1024, 512, 256),
                   lambda t: 4 * tb * f + 2 * f * t * wb + 16 * tb * t + 2 * f * t + 4 * tb * t)
    nb = rows // tb
    live = lambda b, nu: jnp.maximum(jnp.minimum(b, nu[0] - 1), 0)
    in_specs = [
        pl.BlockSpec((tb, f), lambda n, b, be, nu: (live(b, nu), 0)),
        pl.BlockSpec((None, None, f, tn), lambda n, b, be, nu: (layer, be[live(b, nu)], 0, n)),
    ]
    args = [gact, w2]
    if resid is None:
        kern = _down_kernel
    else:
        x, gt, n_lat = resid
        kern = functools.partial(_down_res_kernel, n_lat=n_lat)
        in_specs += [pl.BlockSpec((tb, tn), lambda n, b, be, nu: (b, n)),
                     pl.BlockSpec((2, tn), lambda n, b, be, nu: (0, n))]
        args += [x, gt]
    return pl.pallas_call(
        kern,
        out_shape=jax.ShapeDtypeStruct((rows, d), F32),
        grid_spec=pltpu.PrefetchScalarGridSpec(
            num_scalar_prefetch=2,
            grid=(d // tn, nb),
            in_specs=in_specs,
            out_specs=pl.BlockSpec((tb, tn), lambda n, b, be, nu: (b, n)),
        ),
        compiler_params=_params(("parallel", "arbitrary")),
        name="swiglu_down",
    )(blk_e, n_used, *args)


def _dense_ffn(x, g2, sh, sc, gt, w1, w3, w2, layer, n_lat):
    r = x.shape[0]
    tb = _pick(r, (640, 256))
    nb = r // tb
    blk_e = jnp.zeros((nb,), I32)
    n_used = jnp.full((1,), nb, I32)
    h = _ffn_norm(x, g2, sh, sc, n_lat)
    gact = _swiglu_up(h, blk_e, n_used, w1[:, None], w3[:, None], layer, tb)
    return _swiglu_down(gact, blk_e, n_used, w2[:, None], layer, tb, resid=(x, gt, n_lat))


def _route_kernel(x_ref, g_ref, sh_ref, sc_ref, wr_ref, br_ref, mi_ref, mf_ref, cnt_ref, h_s, run_s, gs_s, sh_s,
                  *, n_lat, n_exp):
    i = pl.program_id(0)
    tm = x_ref.shape[0]

    @pl.when(i == 0)
    def _():
        run_s[...] = jnp.zeros_like(run_s)

    _norm_mod_rows(x_ref, g_ref, sh_ref, sc_ref, h_s, gs_s, sh_s, i * tm, n_lat)
    h = h_s[...]
    w = wr_ref[...]
    h_hi = h.astype(BF16)
    h_lo = (h - h_hi.astype(F32)).astype(BF16)
    w_hi = w.astype(BF16)
    w_lo = (w - w_hi.astype(F32)).astype(BF16)
    logits = (jnp.dot(h_hi, w_hi, preferred_element_type=F32) + jnp.dot(h_hi, w_lo, preferred_element_type=F32)
              + jnp.dot(h_lo, w_hi, preferred_element_type=F32)) + br_ref[...]
    lane = lax.broadcasted_iota(I32, logits.shape, 1).astype(F32)
    l1 = jnp.where(lane < n_exp, logits, -jnp.inf)
    v1 = l1.max(axis=1, keepdims=True)
    e1 = jnp.where(l1 == v1, lane, float(V7X_LANES)).min(axis=1, keepdims=True)
    l2 = jnp.where(lane == e1, -jnp.inf, l1)
    v2 = l2.max(axis=1, keepdims=True)
    e2 = jnp.where(l2 == v2, lane, float(V7X_LANES)).min(axis=1, keepdims=True)
    t = jnp.exp(v2 - v1)
    g1 = 1.0 / (1.0 + t)
    g2 = t / (1.0 + t)

    onehot = jnp.where(jnp.logical_or(lane == e1, lane == e2), 1.0, 0.0)
    below = lax.broadcasted_iota(I32, (tm, tm), 0) > lax.broadcasted_iota(I32, (tm, tm), 1)
    before = jnp.dot(jnp.where(below, 1.0, 0.0).astype(BF16), onehot.astype(BF16),
                     preferred_element_type=F32) + run_s[...]
    r1 = jnp.where(lane == e1, before, 0.0).sum(axis=1, keepdims=True)
    r2 = jnp.where(lane == e2, before, 0.0).sum(axis=1, keepdims=True)
    run_s[...] = run_s[...] + onehot.sum(axis=0, keepdims=True)

    meta = jnp.where(lane == 0, e1, jnp.where(lane == 1, e2, jnp.where(lane == 2, r1, jnp.where(lane == 3, r2, 0.0))))
    mi_ref[...] = meta.astype(I32)
    mf_ref[...] = jnp.where(lane == 0, g1, jnp.where(lane == 1, g2, 0.0))
    cnt_ref[...] = jnp.broadcast_to(run_s[...], cnt_ref.shape)


def _route(x, g2, sh, sc, wr_pad, br_pad, n_lat, n_exp):
    r, d = x.shape
    tm = ROW_TILE
    return pl.pallas_call(
        functools.partial(_route_kernel, n_lat=n_lat, n_exp=n_exp),
        out_shape=(jax.ShapeDtypeStruct((r, V7X_LANES), I32),
                   jax.ShapeDtypeStruct((r, V7X_LANES), F32),
                   jax.ShapeDtypeStruct((V7X_SUBLANES, V7X_LANES), F32)),
        grid=(r // tm,),
        in_specs=[pl.BlockSpec((tm, d), lambda i: (i, 0)),
                  pl.BlockSpec((1, d), lambda i: (0, 0)),
                  pl.BlockSpec((2, d), lambda i: (0, 0)),
                  pl.BlockSpec((2, d), lambda i: (0, 0)),
                  pl.BlockSpec((d, V7X_LANES), lambda i: (0, 0)),
                  pl.BlockSpec((1, V7X_LANES), lambda i: (0, 0))],
        out_specs=(pl.BlockSpec((tm, V7X_LANES), lambda i: (i, 0)),
                   pl.BlockSpec((tm, V7X_LANES), lambda i: (i, 0)),
                   pl.BlockSpec((V7X_SUBLANES, V7X_LANES), lambda i: (0, 0))),
        scratch_shapes=[pltpu.VMEM((tm, d), F32), pltpu.VMEM((1, V7X_LANES), F32)] + _norm_scratch(d),
        compiler_params=_params(("arbitrary",)),
        name="route",
    )(x, g2, sh, sc, wr_pad, br_pad)


def _row_copy(src, s_row, dst, d_row, sem):
    return pltpu.make_async_copy(src.at[pl.ds(s_row, 1), :], dst.at[pl.ds(d_row, 1), :], sem)


def _dispatch_kernel(d1_ref, d2_ref, x_ref, g_ref, sh_ref, sc_ref, xb_in_ref, xb_ref, h_s, pk_s, sem, gs_s, sh_s,
                     *, n_lat):
    del xb_in_ref
    i = pl.program_id(0)
    tm, d = x_ref.shape
    _norm_mod_rows(x_ref, g_ref, sh_ref, sc_ref, h_s, gs_s, sh_s, i * tm, n_lat)
    pk_s[...] = _pack_bf16_pairs(h_s[:, :d // 2], h_s[:, d // 2:])

    def start(r, c):
        _row_copy(pk_s, r, xb_ref, d1_ref[0, r], sem.at[0]).start()
        _row_copy(pk_s, r, xb_ref, d2_ref[0, r], sem.at[1]).start()
        return c

    def wait(r, c):
        _row_copy(pk_s, r, xb_ref, d1_ref[0, r], sem.at[0]).wait()
        _row_copy(pk_s, r, xb_ref, d2_ref[0, r], sem.at[1]).wait()
        return c

    lax.fori_loop(0, tm, start, 0, unroll=ROW_DMA_UNROLL)
    lax.fori_loop(0, tm, wait, 0, unroll=ROW_DMA_UNROLL)


def _dispatch(x, g2, sh, sc, d1, d2, n_rows, n_lat):
    r, d = x.shape
    tm = ROW_TILE
    nt = r // tm
    smem_rows = pl.BlockSpec((None, 1, tm), lambda i: (i, 0, 0), memory_space=pltpu.SMEM)
    return pl.pallas_call(
        functools.partial(_dispatch_kernel, n_lat=n_lat),
        out_shape=jax.ShapeDtypeStruct((n_rows, d // 2), U32),
        grid=(nt,),
        in_specs=[smem_rows, smem_rows,
                  pl.BlockSpec((tm, d), lambda i: (i, 0)),
                  pl.BlockSpec((1, d), lambda i: (0, 0)),
                  pl.BlockSpec((2, d), lambda i: (0, 0)),
                  pl.BlockSpec((2, d), lambda i: (0, 0)),
                  pl.BlockSpec(memory_space=pl.ANY)],
        out_specs=pl.BlockSpec(memory_space=pl.ANY),
        scratch_shapes=[pltpu.VMEM((tm, d), F32), pltpu.VMEM((tm, d // 2), U32),
                        pltpu.SemaphoreType.DMA((2,))] + _norm_scratch(d),
        input_output_aliases={6: 0},
        compiler_params=_params(("arbitrary",)),
        name="moe_dispatch",
    )(d1.reshape(nt, 1, tm), d2.reshape(nt, 1, tm), x, g2, sh, sc, jnp.zeros((n_rows, d // 2), U32))


def _combine_kernel(d1_ref, d2_ref, mf_ref, x_ref, gt_ref, yb_ref, o_ref, buf, sem, *, n_lat):
    i = pl.program_id(0)
    tm = x_ref.shape[0]

    def start(r, c):
        _row_copy(yb_ref, d1_ref[0, r], buf.at[0], r, sem.at[0]).start()
        _row_copy(yb_ref, d2_ref[0, r], buf.at[1], r, sem.at[1]).start()
        return c

    def wait(r, c):
        _row_copy(yb_ref, d1_ref[0, r], buf.at[0], r, sem.at[0]).wait()
        _row_copy(yb_ref, d2_ref[0, r], buf.at[1], r, sem.at[1]).wait()
        return c

    lax.fori_loop(0, tm, start, 0, unroll=ROW_DMA_UNROLL)
    lax.fori_loop(0, tm, wait, 0, unroll=ROW_DMA_UNROLL)
    f = mf_ref[:, 0:1] * buf[0] + mf_ref[:, 1:2] * buf[1]
    o_ref[...] = x_ref[...] + _row_select(gt_ref, i * tm, tm, n_lat) * f


def _combine(x, gt, yb, d1, d2, mf, n_lat):
    r, d = x.shape
    tm = ROW_TILE
    nt = r // tm
    smem_rows = pl.BlockSpec((None, 1, tm), lambda i: (i, 0, 0), memory_space=pltpu.SMEM)
    return pl.pallas_call(
        functools.partial(_combine_kernel, n_lat=n_lat),
        out_shape=jax.ShapeDtypeStruct((r, d), F32),
        grid=(nt,),
        in_specs=[smem_rows, smem_rows,
                  pl.BlockSpec((tm, V7X_LANES), lambda i: (i, 0)),
                  pl.BlockSpec((tm, d), lambda i: (i, 0)),
                  pl.BlockSpec((2, d), lambda i: (0, 0)),
                  pl.BlockSpec(memory_space=pl.ANY)],
        out_specs=pl.BlockSpec((tm, d), lambda i: (i, 0)),
        scratch_shapes=[pltpu.VMEM((2, tm, d), F32), pltpu.SemaphoreType.DMA((2,))],
        compiler_params=_params(("arbitrary",)),
        name="moe_combine",
    )(d1.reshape(nt, 1, tm), d2.reshape(nt, 1, tm), mf, x, gt, yb)


def _moe_ffn(x, g2, sh, sc, gt, wr, br, w1, w3, w2, layer, n_lat):
    r, d = x.shape
    n_exp = wr.shape[1]
    blk = MOE_BLOCK
    wr_pad = jnp.zeros((d, V7X_LANES), F32).at[:, :n_exp].set(wr)
    br_pad = jnp.zeros((1, V7X_LANES), F32).at[0, :n_exp].set(br)
    mi, mf, cnt = _route(x, g2, sh, sc, wr_pad, br_pad, n_lat, n_exp)

    counts = cnt[0, :n_exp].astype(I32)
    padded = (counts + blk - 1) // blk * blk
    pad_end = jnp.cumsum(padded)
    pad_start = pad_end - padded
    d1 = pad_start[mi[:, 0]] + mi[:, 2]
    d2 = pad_start[mi[:, 1]] + mi[:, 3]
    n_rows = -(-(TOP_K * r + n_exp * (blk - 1)) // blk) * blk
    nb = n_rows // blk
    blk_e = jnp.minimum(jnp.searchsorted(pad_end, jnp.arange(nb, dtype=I32) * blk, side="right"),
                        n_exp - 1).astype(I32)
    n_used = (pad_end[-1:] // blk).astype(I32)

    xb = _dispatch(x, g2, sh, sc, d1, d2, n_rows, n_lat)
    gact = _swiglu_up(xb, blk_e, n_used, w1, w3, layer, blk)
    yb = _swiglu_down(gact, blk_e, n_used, w2, layer, blk)
    return _combine(x, gt, yb, d1, d2, mf, n_lat)


def _rope_tables(n_lat, n_ctx):
    n_rows = n_lat // GRID_W
    row = jnp.repeat(jnp.arange(n_rows, dtype=F32), GRID_W)
    col = jnp.tile(jnp.arange(GRID_W, dtype=F32), n_rows)
    axis_dim = HEAD_DIM // 2
    inv_freq = ROPE_THETA ** (-jnp.arange(0, axis_dim, 2, dtype=F32) / axis_dim)
    ang_r = row[:, None] * inv_freq[None, :]
    ang_c = col[:, None] * inv_freq[None, :]
    cr, sr, cc, sc = jnp.cos(ang_r), jnp.sin(ang_r), jnp.cos(ang_c), jnp.sin(ang_c)
    cos_t = jnp.concatenate([cr, cr, cc, cc], axis=1)
    sin_t = jnp.concatenate([-sr, sr, -sc, sc], axis=1)
    cos_t = jnp.concatenate([cos_t, jnp.ones((n_ctx, HEAD_DIM), F32)], axis=0)
    sin_t = jnp.concatenate([sin_t, jnp.zeros((n_ctx, HEAD_DIM), F32)], axis=0)
    return cos_t, sin_t


def kernel(x, c, ctx, c_ctx, w_ada, b_ada, g_norm1, w_in, conv_w, conv_b, conv_ln_g, conv_ln_b, conv_pw, conv_pw_b, sgu_ln_g, sgu_ln_b, sgu_w, sgu_b, swa_q_g, swa_k_g, swa_sink, glb_q_g, glb_k_g, g_branch, w_out, g_norm2, ffn_w1, ffn_w3, ffn_w2, router_w, router_b, exp_w1, exp_w3, exp_w2):
    batch, n_lat, d = x.shape
    n_ctx = ctx.shape[1]
    depth = w_ada.shape[0]
    gw = d // N_GROUPS
    assert batch == 1 and n_lat % n_ctx == 0 and n_ctx % ROW_TILE == 0 and n_lat % GRID_W == 0
    assert conv_w.shape[1] // 2 < CONV_HALO and w_in.shape[2] == 4 * 2 * gw

    xs = jnp.concatenate([x[0], ctx[0]], axis=0)
    mods = _ada(jnp.stack([c[0], c_ctx], axis=1), w_ada, b_ada)
    cos_t, sin_t = _rope_tables(n_lat, n_ctx)
    row2 = lambda v: v.reshape(1, -1)

    for l in range(depth):
        sh1, sc1, gt1, sh2, sc2, gt2 = (mods[l, :, k * d:(k + 1) * d] for k in range(6))

        p = _inproj(xs, row2(g_norm1[l]), sh1, sc1, w_in[l].astype(BF16), n_lat)
        y_conv = _conv_group(p, conv_w[l], row2(conv_b[l]), row2(conv_ln_g[l]), row2(conv_ln_b[l]),
                             conv_pw[l].astype(BF16), row2(conv_pw_b[l]), n_lat)
        y_sgu = _sgu_group(p, row2(sgu_ln_g[l]), row2(sgu_ln_b[l]), sgu_w[l].astype(BF16), sgu_b[l].T)
        q, k, v, qt, vt, kn = _prep(p, jnp.stack([swa_q_g[l], glb_q_g[l]])[:, None, :],
                        jnp.stack([swa_k_g[l], glb_k_g[l]])[:, None, :], cos_t, sin_t, first_block=2)
        y_swa = _swa(q, k, v, swa_sink[l], n_lat, grp=0)
        kmax = jnp.sqrt(jnp.max(kn[1, :, 0, :N_KV_HEADS], axis=0)) * KEY_NORM_MARGIN
        y_glb = _glb(qt, k, vt, kmax, n_lat, grp=1)
        xs = _outproj((y_conv, y_sgu, y_swa, y_glb), row2(g_branch[l]), w_out[l].astype(BF16), xs, gt1, n_lat)

        if l % 2 == 0:
            xs = _dense_ffn(xs, row2(g_norm2[l]), sh2, sc2, gt2, ffn_w1, ffn_w3, ffn_w2, l // 2, n_lat)
        else:
            xs = _moe_ffn(xs, row2(g_norm2[l]), sh2, sc2, gt2, router_w[l // 2], router_b[l // 2],
                          exp_w1, exp_w3, exp_w2, l // 2, n_lat)
    return xs[:n_lat][None]
```

```python
import functools
import math

import jax
import jax.numpy as jnp
from jax import lax
from jax.experimental import pallas as pl
from jax.experimental.pallas import tpu as pltpu

F32 = jnp.float32
BF16 = jnp.bfloat16
I32 = jnp.int32

HEAD_DIM = 128
N_GROUPS = 4
N_KV_HEADS = 2
GRID_W = 64
WINDOW = 128
ROPE_THETA = 10000.0
TOP_K = 2
MOE_BLOCK = 512
EPS = 1e-6
NEG = -1e30
SCALE = HEAD_DIM ** -0.5
Q_PRESCALE = SCALE * math.log2(math.e)
EXP2_SAFE_SHIFT = 60.0
KEY_NORM_MARGIN = 1.01

V7X_VMEM_BYTES = 64 * 1024 * 1024
V7X_LANES = 128
V7X_SUBLANES = 8
VMEM_BUDGET = V7X_VMEM_BYTES - 8 * 1024 * 1024

ROW_TILE = 256
CONV_HALO = 16
NORM_CHUNK = 16
NORM_UNROLL = 8
ROW_DMA_UNROLL = 8


def _pick(n, cands):
    for c in cands:
        if n % c == 0:
            return c
    raise ValueError(f"no tile in {cands} divides {n}")


def _params(sem, vmem=VMEM_BUDGET):
    return pltpu.CompilerParams(dimension_semantics=sem, vmem_limit_bytes=vmem)


def _sigmoid(x):
    return 1.0 / (1.0 + jnp.exp(-x))


def _silu(x):
    return x * _sigmoid(x)


def _gelu_tanh(x):
    c = math.sqrt(2.0 / math.pi)
    return 0.5 * x * (1.0 + jnp.tanh(c * (x + 0.044715 * (x * x * x))))


def _rms(x):
    return x * lax.rsqrt(jnp.mean(x * x, axis=-1, keepdims=True) + EPS)


def _layer_norm(x, g, b):
    xc = x - jnp.mean(x, axis=-1, keepdims=True)
    return xc * lax.rsqrt(jnp.mean(xc * xc, axis=-1, keepdims=True) + EPS) * g + b


def _row_select(vec2_ref, row0, tm, n_lat):
    rows = row0 + lax.broadcasted_iota(I32, (tm, 1), 0)
    return jnp.where(rows >= n_lat, vec2_ref[1:2, :], vec2_ref[0:1, :])


def _norm_mod(x, g, shift, scale):
    return _rms(x) * g * (1.0 + scale) + shift


def _norm_scratch(d):
    return [pltpu.VMEM((2 * V7X_SUBLANES, d), F32), pltpu.VMEM((2 * V7X_SUBLANES, d), F32)]


def _norm_mod_rows(x_ref, g_ref, sh_ref, sc_ref, out_ref, gs_s, sh_s, row0, n_lat):
    tm, d = x_ref.shape
    sub = V7X_SUBLANES
    for t in range(2):
        gs_s[t * sub:(t + 1) * sub, :] = jnp.broadcast_to(g_ref[...] * (1.0 + sc_ref[t:t + 1, :]), (sub, d))
        sh_s[t * sub:(t + 1) * sub, :] = jnp.broadcast_to(sh_ref[t:t + 1, :], (sub, d))

    def body(c, carry):
        r0 = pl.multiple_of(c * NORM_CHUNK, NORM_CHUNK)
        t0 = pl.multiple_of(jnp.where(row0 + r0 >= n_lat, sub, 0), sub)
        x = x_ref[pl.ds(r0, NORM_CHUNK), :].reshape(NORM_CHUNK // sub, sub, d)
        y = _rms(x) * gs_s[pl.ds(t0, sub), :] + sh_s[pl.ds(t0, sub), :]
        out_ref[pl.ds(r0, NORM_CHUNK), :] = y.reshape(NORM_CHUNK, d).astype(out_ref.dtype)
        return carry

    lax.fori_loop(0, tm // NORM_CHUNK, body, 0, unroll=NORM_UNROLL)


def _ada_kernel(s_ref, w_ref, b_ref, o_ref):
    d, tn = w_ref.shape
    kc = 64

    def body(c, acc):
        a0, a1 = acc
        k0 = pl.multiple_of(c * kc, kc)
        w = w_ref[pl.ds(k0, kc), :]
        s = _silu(s_ref[pl.ds(k0, kc), :])
        a0 = a0 + (w * s[:, 0:1]).reshape(kc // V7X_SUBLANES, V7X_SUBLANES, tn).sum(axis=0)
        a1 = a1 + (w * s[:, 1:2]).reshape(kc // V7X_SUBLANES, V7X_SUBLANES, tn).sum(axis=0)
        return a0, a1

    z = jnp.zeros((V7X_SUBLANES, tn), F32)
    a0, a1 = lax.fori_loop(0, d // kc, body, (z, z))
    o_ref[0:1, :] = a0.sum(axis=0, keepdims=True) + b_ref[...]
    o_ref[1:2, :] = a1.sum(axis=0, keepdims=True) + b_ref[...]


def _ada(cond, w_ada, b_ada):
    depth, d, n = w_ada.shape
    tn = _pick(n, (1024, 512, 256, 128))
    return pl.pallas_call(
        _ada_kernel,
        out_shape=jax.ShapeDtypeStruct((depth, 2, n), F32),
        grid=(depth, n // tn),
        in_specs=[
            pl.BlockSpec((d, 2), lambda l, j: (0, 0)),
            pl.BlockSpec((None, d, tn), lambda l, j: (l, 0, j)),
            pl.BlockSpec((None, 1, tn), lambda l, j: (l, 0, j)),
        ],
        out_specs=pl.BlockSpec((None, 2, tn), lambda l, j: (l, 0, j)),
        compiler_params=_params(("parallel", "parallel")),
        name="ada",
    )(cond, w_ada, b_ada.reshape(depth, 1, n))


def _weight_spec(k, n, itemsize, other_bytes):
    if other_bytes + k * n * itemsize <= VMEM_BUDGET:
        return n, pl.BlockSpec((k, n), lambda i, j: (0, 0), pipeline_mode=pl.Buffered(1))
    tn = _pick(n, (1024, 512, 256))
    return tn, pl.BlockSpec((k, tn), lambda i, j: (0, j))


def _inproj_kernel(x_ref, g_ref, sh_ref, sc_ref, w_ref, o_ref, h_s, gs_s, sh_s, *, n_lat):
    i = pl.program_id(0)
    tm = x_ref.shape[0]

    @pl.when(pl.program_id(1) == 0)
    def _():
        _norm_mod_rows(x_ref, g_ref, sh_ref, sc_ref, h_s, gs_s, sh_s, i * tm, n_lat)

    o_ref[...] = jnp.dot(h_s[...], w_ref[...], preferred_element_type=F32)


def _inproj(x, g, sh, sc, w, n_lat):
    r, d = x.shape
    n = w.shape[1]
    tm = _pick(r, (640, 256))
    tn, w_spec = _weight_spec(d, n, w.dtype.itemsize, 2 * tm * d * 4 + 2 * tm * n * 4 + tm * d * 2)
    return pl.pallas_call(
        functools.partial(_inproj_kernel, n_lat=n_lat),
        out_shape=jax.ShapeDtypeStruct((r, n), F32),
        grid=(r // tm, n // tn),
        in_specs=[
            pl.BlockSpec((tm, d), lambda i, j: (i, 0)),
            pl.BlockSpec((1, d), lambda i, j: (0, 0)),
            pl.BlockSpec((2, d), lambda i, j: (0, 0)),
            pl.BlockSpec((2, d), lambda i, j: (0, 0)),
            w_spec,
        ],
        out_specs=pl.BlockSpec((tm, tn), lambda i, j: (i, j)),
        scratch_shapes=[pltpu.VMEM((tm, d), BF16)] + _norm_scratch(d),
        compiler_params=_params(("parallel", "arbitrary")),
        name="inproj",
    )(x, g, sh, sc, w)


def _conv_kernel(pm_ref, pp_ref, pn_ref, cw_ref, cb_ref, lg_ref, lb_ref, pw_ref, pb_ref, o_ref, ext_s, sft_s,
                 *, n_lat_tiles, n_tiles):
    i = pl.program_id(0)
    tm, gw = o_ref.shape
    kw = cw_ref.shape[0]

    def glu(p):
        return p[:, :gw] * _sigmoid(p[:, gw:])

    prev_ok = jnp.logical_and(i != 0, i != n_lat_tiles)
    next_ok = jnp.logical_and(i != n_lat_tiles - 1, i != n_tiles - 1)
    ext_s[0:CONV_HALO, :] = jnp.where(prev_ok, glu(pp_ref[...]), 0.0)
    ext_s[CONV_HALO:CONV_HALO + tm, :] = glu(pm_ref[...])
    ext_s[CONV_HALO + tm:, :] = jnp.where(next_ok, glu(pn_ref[...]), 0.0)

    sub = V7X_SUBLANES
    span = sft_s.shape[1]
    for b in range(1, sub):
        sft_s[b - 1] = ext_s[b:b + span, :]
    acc = jnp.zeros((tm, gw), F32) + cb_ref[...]
    for k in range(kw):
        a, b = divmod(CONV_HALO - kw // 2 + k, sub)
        tap = ext_s[a * sub:a * sub + tm, :] if b == 0 else sft_s[b - 1, a * sub:a * sub + tm, :]
        acc = acc + cw_ref[k:k + 1, :] * tap
    y = _silu(_layer_norm(acc, lg_ref[...], lb_ref[...]))
    o_ref[...] = jnp.dot(y.astype(BF16), pw_ref[...], preferred_element_type=F32) + pb_ref[...]


def _conv_group(p, cw, cb, lg, lb, pw, pb, n_lat):
    r = p.shape[0]
    kw, gw = cw.shape
    tm = ROW_TILE
    n_tiles = r // tm
    hb = tm // CONV_HALO
    last_hb = r // CONV_HALO - 1
    return pl.pallas_call(
        functools.partial(_conv_kernel, n_lat_tiles=n_lat // tm, n_tiles=n_tiles),
        out_shape=jax.ShapeDtypeStruct((r, gw), F32),
        grid=(n_tiles,),
        in_specs=[
            pl.BlockSpec((tm, 2 * gw), lambda i: (i, 0)),
            pl.BlockSpec((CONV_HALO, 2 * gw), lambda i: (jnp.maximum(i * hb - 1, 0), 0)),
            pl.BlockSpec((CONV_HALO, 2 * gw), lambda i: (jnp.minimum((i + 1) * hb, last_hb), 0)),
            pl.BlockSpec((kw, gw), lambda i: (0, 0)),
            pl.BlockSpec((1, gw), lambda i: (0, 0)),
            pl.BlockSpec((1, gw), lambda i: (0, 0)),
            pl.BlockSpec((1, gw), lambda i: (0, 0)),
            pl.BlockSpec((gw, gw), lambda i: (0, 0)),
            pl.BlockSpec((1, gw), lambda i: (0, 0)),
        ],
        out_specs=pl.BlockSpec((tm, gw), lambda i: (i, 0)),
        scratch_shapes=[pltpu.VMEM((tm + 2 * CONV_HALO, gw), F32),
                        pltpu.VMEM((V7X_SUBLANES - 1, tm + 2 * CONV_HALO - V7X_SUBLANES, gw), F32)],
        compiler_params=_params(("parallel",)),
        name="conv_group",
    )(p, p, p, cw, cb, lg, lb, pw, pb)


def _sgu_kernel(p_ref, lg_ref, lb_ref, ws_ref, bs_ref, o_ref):
    tm, gw = o_ref.shape
    n_h, ch, _ = ws_ref.shape
    hd = gw // n_h
    z = _gelu_tanh(p_ref[...])
    u = z[:, :gw]
    v = _layer_norm(z[:, gw:], lg_ref[...], lb_ref[...]).astype(BF16)
    for c in range(tm // ch):
        rows = slice(c * ch, (c + 1) * ch)
        parts = []
        for h in range(n_h):
            s = jnp.dot(ws_ref[h], v[rows, h * hd:(h + 1) * hd], preferred_element_type=F32)
            parts.append(s + bs_ref[:, h:h + 1])
        o_ref[rows, :] = u[rows, :] * jnp.concatenate(parts, axis=1)


def _sgu_group(p, lg, lb, ws, bs_t):
    r = p.shape[0]
    gw = lg.shape[1]
    n_h, ch, _ = ws.shape
    tm = ROW_TILE
    return pl.pallas_call(
        _sgu_kernel,
        out_shape=jax.ShapeDtypeStruct((r, gw), F32),
        grid=(r // tm,),
        in_specs=[
            pl.BlockSpec((tm, 2 * gw), lambda i: (i, 1)),
            pl.BlockSpec((1, gw), lambda i: (0, 0)),
            pl.BlockSpec((1, gw), lambda i: (0, 0)),
            pl.BlockSpec((n_h, ch, ch), lambda i: (0, 0, 0)),
            pl.BlockSpec((ch, n_h), lambda i: (0, 0)),
        ],
        out_specs=pl.BlockSpec((tm, gw), lambda i: (i, 0)),
        compiler_params=_params(("parallel",)),
        name="sgu_group",
    )(p, lg, lb, ws, bs_t)


def _prep_kernel(p_ref, qg_ref, kg_ref, cos_ref, sin_ref, q_ref, k_ref, v_ref, *kn_ref, feature_major):
    tm = p_ref.shape[0]
    kvw = k_ref.shape[1]
    qw = p_ref.shape[1] - 2 * kvw
    cos = cos_ref[...]
    sin = sin_ref[...]
    lane = lax.broadcasted_iota(I32, (tm, HEAD_DIM), 1)
    low = (lane & (HEAD_DIM // 4)) == 0

    def head(x, g, mult):
        y = _rms(x) * g
        partner = jnp.where(low, pltpu.roll(y, HEAD_DIM - HEAD_DIM // 4, 1), pltpu.roll(y, HEAD_DIM // 4, 1))
        return (y * cos + partner * sin) * mult

    def put(ref, h, val):
        cols = slice(h * HEAD_DIM, (h + 1) * HEAD_DIM)
        if feature_major:
            ref[cols, :] = val.astype(BF16).T
        else:
            ref[:, cols] = val.astype(BF16)

    for h in range(qw // HEAD_DIM):
        put(q_ref, h, head(p_ref[:, h * HEAD_DIM:(h + 1) * HEAD_DIM], qg_ref[...], Q_PRESCALE))
    kn = None
    for h in range(kvw // HEAD_DIM):
        kh = head(p_ref[:, qw + h * HEAD_DIM:qw + (h + 1) * HEAD_DIM], kg_ref[...], 1.0)
        k_ref[:, h * HEAD_DIM:(h + 1) * HEAD_DIM] = kh.astype(BF16)
        put(v_ref, h, p_ref[:, qw + kvw + h * HEAD_DIM:qw + kvw + (h + 1) * HEAD_DIM])
        if feature_major:
            kn_lane = lax.broadcasted_iota(I32, kn_ref[0].shape, 1)
            ksq = jnp.sum(kh * kh, axis=1, keepdims=True).max(axis=0, keepdims=True)
            kn = jnp.where(kn_lane == h, ksq, jnp.zeros(kn_ref[0].shape, F32) if kn is None else kn)
    if feature_major:
        kn_ref[0][...] = kn


def _prep(p, qg, kg, cos_t, sin_t, block, feature_major):
    r = p.shape[0]
    qw = N_GROUPS * HEAD_DIM
    kvw = N_KV_HEADS * HEAD_DIM
    tm = ROW_TILE
    if feature_major:
        shape = lambda w: jax.ShapeDtypeStruct((w, r), BF16)
        spec = lambda w: pl.BlockSpec((w, tm), lambda i: (0, i))
    else:
        shape = lambda w: jax.ShapeDtypeStruct((r, w), BF16)
        spec = lambda w: pl.BlockSpec((tm, w), lambda i: (i, 0))
    out_shape = [shape(qw), jax.ShapeDtypeStruct((r, kvw), BF16), shape(kvw)]
    out_specs = [spec(qw), pl.BlockSpec((tm, kvw), lambda i: (i, 0)), spec(kvw)]
    if feature_major:
        out_shape.append(jax.ShapeDtypeStruct((r // tm, V7X_SUBLANES, V7X_LANES), F32))
        out_specs.append(pl.BlockSpec((None, V7X_SUBLANES, V7X_LANES), lambda i: (i, 0, 0)))
    return pl.pallas_call(
        functools.partial(_prep_kernel, feature_major=feature_major),
        out_shape=tuple(out_shape),
        grid=(r // tm,),
        in_specs=[
            pl.BlockSpec((tm, qw + 2 * kvw), lambda i: (i, block)),
            pl.BlockSpec((1, HEAD_DIM), lambda i: (0, 0)),
            pl.BlockSpec((1, HEAD_DIM), lambda i: (0, 0)),
            pl.BlockSpec((tm, HEAD_DIM), lambda i: (i, 0)),
            pl.BlockSpec((tm, HEAD_DIM), lambda i: (i, 0)),
        ],
        out_specs=tuple(out_specs),
        compiler_params=_params(("parallel",)),
        name="qkv_prep_t" if feature_major else "qkv_prep",
    )(p, qg, kg, cos_t, sin_t)


def _nt_dot(a, b):
    return lax.dot_general(a, b, (((1,), (1,)), ((), ())), preferred_element_type=F32)


def _swa_kernel(sink_ref, qt_ref, km_ref, kp_ref, kn_ref, vm_ref, vp_ref, vn_ref, kc_ref, vc_ref, o_ref,
                *, n_lat):
    i = pl.program_id(0)
    tq = qt_ref.shape[1]
    hb = kp_ref.shape[0]
    n_rep = (qt_ref.shape[0] // HEAD_DIM) // N_KV_HEADS
    nk = tq + 2 * hb
    kj = lax.broadcasted_iota(I32, (nk, tq), 0) - hb
    qi = lax.broadcasted_iota(I32, (nk, tq), 1)
    kglob = i * tq + kj
    ok1 = (jnp.abs(kj - qi) <= WINDOW) & (kglob >= 0) & (kglob < n_lat) & (i * tq < n_lat)
    ok = jnp.concatenate([ok1.astype(F32)] * n_rep, axis=1) > 0.5
    q_lane = lax.broadcasted_iota(I32, (1, n_rep * tq), 1)
    for j in range(N_KV_HEADS):
        kv = slice(j * HEAD_DIM, (j + 1) * HEAD_DIM)
        q2t = jnp.concatenate(
            [qt_ref[(j * n_rep + g) * HEAD_DIM:(j * n_rep + g + 1) * HEAD_DIM, :] for g in range(n_rep)], axis=1)
        kw = jnp.concatenate([kp_ref[:, kv], km_ref[:, kv], kn_ref[:, kv]], axis=0)
        vwt = jnp.concatenate([vp_ref[kv, :], vm_ref[kv, :], vn_ref[kv, :]], axis=1)
        s_w = jnp.where(ok, jnp.dot(kw, q2t, preferred_element_type=F32), NEG)
        s_c = jnp.dot(kc_ref[:, kv], q2t, preferred_element_type=F32)
        sk = jnp.zeros((1, n_rep * tq), F32)
        for g in range(n_rep):
            sk = jnp.where(q_lane >= g * tq, sink_ref[j * n_rep + g] * math.log2(math.e), sk)
        m = jnp.maximum(jnp.maximum(s_w.max(axis=0, keepdims=True), s_c.max(axis=0, keepdims=True)), sk)
        p_w = jnp.exp2(s_w - m)
        p_c = jnp.exp2(s_c - m)
        den = jnp.exp2(sk - m) + p_w.sum(axis=0, keepdims=True) + p_c.sum(axis=0, keepdims=True)
        ot = (jnp.dot(vwt, p_w.astype(BF16), preferred_element_type=F32)
              + jnp.dot(vc_ref[kv, :], p_c.astype(BF16), preferred_element_type=F32)) / den
        o = ot.T
        for g in range(n_rep):
            h = j * n_rep + g
            o_ref[:, h * HEAD_DIM:(h + 1) * HEAD_DIM] = o[g * tq:(g + 1) * tq, :]


def _swa(qt, k, vt, sink, n_lat):
    qw = qt.shape[0]
    r, kvw = k.shape
    n_ctx = r - n_lat
    tq = ROW_TILE
    hb = WINDOW
    per = tq // hb
    last_hb = r // hb - 1
    prev_b = lambda i: jnp.maximum(i * per - 1, 0)
    next_b = lambda i: jnp.minimum((i + 1) * per, last_hb)
    return pl.pallas_call(
        functools.partial(_swa_kernel, n_lat=n_lat),
        out_shape=jax.ShapeDtypeStruct((r, qw), F32),
        grid=(r // tq,),
        in_specs=[
            pl.BlockSpec(memory_space=pltpu.SMEM),
            pl.BlockSpec((qw, tq), lambda i: (0, i)),
            pl.BlockSpec((tq, kvw), lambda i: (i, 0)),
            pl.BlockSpec((hb, kvw), lambda i: (prev_b(i), 0)),
            pl.BlockSpec((hb, kvw), lambda i: (next_b(i), 0)),
            pl.BlockSpec((kvw, tq), lambda i: (0, i)),
            pl.BlockSpec((kvw, hb), lambda i: (0, prev_b(i))),
            pl.BlockSpec((kvw, hb), lambda i: (0, next_b(i))),
            pl.BlockSpec((n_ctx, kvw), lambda i: (n_lat // n_ctx, 0)),
            pl.BlockSpec((kvw, n_ctx), lambda i: (0, n_lat // n_ctx)),
        ],
        out_specs=pl.BlockSpec((tq, qw), lambda i: (i, 0)),
        compiler_params=_params(("parallel",)),
        name="window_attn",
    )(sink, qt, k, k, k, vt, vt, vt, k, vt)


def _glb_kernel(kmax_ref, qt_ref, k_ref, vt_ref, o_ref, acc_s, l_s, *, n_lat, tk, unroll):
    j = pl.program_id(0)
    i = pl.program_id(1)
    tq = qt_ref.shape[1]
    n_rep = qt_ref.shape[0] // HEAD_DIM
    nq = n_rep * tq
    n_ctx = k_ref.shape[0] - n_lat
    q2t = jnp.concatenate([qt_ref[g * HEAD_DIM:(g + 1) * HEAD_DIM, :] for g in range(n_rep)], axis=1)
    qf = q2t.astype(F32)
    bound = jnp.sqrt(jnp.sum(qf * qf, axis=0, keepdims=True)) * kmax_ref[j]

    def scores(k0, size):
        return jnp.dot(k_ref[pl.ds(k0, size), :], q2t, preferred_element_type=F32)

    def shifted_by_bound():
        def add(k0, size):
            p = jnp.exp2(scores(k0, size) - bound)
            l_s[...] += p.sum(axis=0, keepdims=True)
            acc_s[...] += jnp.dot(vt_ref[:, pl.ds(k0, size)], p.astype(BF16), preferred_element_type=F32)

        acc_s[...] = jnp.zeros_like(acc_s)
        l_s[...] = jnp.zeros_like(l_s)
        add(n_lat, n_ctx)

        @pl.when(i * tq < n_lat)
        def _():
            def body(c, carry):
                for u in range(unroll):
                    add(pl.multiple_of((c * unroll + u) * tk, tk), tk)
                return carry

            lax.fori_loop(0, n_lat // tk // unroll, body, 0)

    def shifted_by_running_max():
        def update(s, vt, m, l):
            m_new = jnp.maximum(m, s.max(axis=0, keepdims=True))
            alpha = jnp.exp2(m - m_new)
            p = jnp.exp2(s - m_new)
            l = alpha * l + p.sum(axis=0, keepdims=True)
            acc_s[...] = alpha * acc_s[...] + jnp.dot(vt, p.astype(BF16), preferred_element_type=F32)
            return m_new, l

        acc_s[...] = jnp.zeros_like(acc_s)
        first = update(scores(n_lat, n_ctx), vt_ref[:, n_lat:],
                       jnp.full((1, nq), NEG, F32), jnp.zeros((1, nq), F32))

        def body(c, carry):
            k0 = pl.multiple_of(c * tk, tk)
            return update(scores(k0, tk), vt_ref[:, pl.ds(k0, tk)], *carry)

        _, l = lax.fori_loop(0, jnp.where(i * tq < n_lat, n_lat // tk, 0), body, first)
        l_s[...] = l

    lax.cond(jnp.max(bound) <= EXP2_SAFE_SHIFT, shifted_by_bound, shifted_by_running_max)
    o = (acc_s[...] / l_s[...]).T
    for g in range(n_rep):
        o_ref[:, g * HEAD_DIM:(g + 1) * HEAD_DIM] = o[g * tq:(g + 1) * tq, :]


def _glb(qt, k, vt, kmax, n_lat):
    r, kvw = k.shape
    qw = qt.shape[0]
    tq = ROW_TILE
    gq = qw // N_KV_HEADS
    tk = _pick(n_lat, (2048, 1024, 512, 256))
    unroll = 2 if (n_lat // tk) % 2 == 0 else 1
    nq = gq // HEAD_DIM * tq
    return pl.pallas_call(
        functools.partial(_glb_kernel, n_lat=n_lat, tk=tk, unroll=unroll),
        out_shape=jax.ShapeDtypeStruct((r, qw), F32),
        grid=(N_KV_HEADS, r // tq),
        in_specs=[
            pl.BlockSpec(memory_space=pltpu.SMEM),
            pl.BlockSpec((gq, tq), lambda j, i: (j, i)),
            pl.BlockSpec((r, HEAD_DIM), lambda j, i: (0, j)),
            pl.BlockSpec((HEAD_DIM, r), lambda j, i: (j, 0)),
        ],
        out_specs=pl.BlockSpec((tq, gq), lambda j, i: (i, j)),
        scratch_shapes=[pltpu.VMEM((HEAD_DIM, nq), F32), pltpu.VMEM((1, nq), F32)],
        compiler_params=_params(("parallel", "parallel")),
        name="global_attn",
    )(kmax, qt, k, vt)


def _outproj_kernel(y0_ref, y1_ref, y2_ref, y3_ref, gb_ref, w_ref, x_ref, gt_ref, o_ref, yn_s, *, n_lat):
    i = pl.program_id(0)
    tm = x_ref.shape[0]

    @pl.when(pl.program_id(1) == 0)
    def _():
        for g, y_ref in enumerate((y0_ref, y1_ref, y2_ref, y3_ref)):
            gw = y_ref.shape[1]
            cols = slice(g * gw, (g + 1) * gw)
            yn_s[:, cols] = (_rms(y_ref[...]) * gb_ref[:, cols]).astype(BF16)

    gate = _row_select(gt_ref, i * tm, tm, n_lat)
    o_ref[...] = x_ref[...] + gate * jnp.dot(yn_s[...], w_ref[...], preferred_element_type=F32)


def _outproj(ys, gb, w, x, gt, n_lat):
    r, d = x.shape
    gw = ys[0].shape[1]
    tm = _pick(r, (640, 256))
    tn, w_spec = _weight_spec(d, d, w.dtype.itemsize, 2 * tm * d * 4 * 3 + tm * d * 2)
    ysp = pl.BlockSpec((tm, gw), lambda i, j: (i, 0))
    return pl.pallas_call(
        functools.partial(_outproj_kernel, n_lat=n_lat),
        out_shape=jax.ShapeDtypeStruct((r, d), F32),
        grid=(r // tm, d // tn),
        in_specs=[ysp, ysp, ysp, ysp,
                  pl.BlockSpec((1, d), lambda i, j: (0, 0)),
                  w_spec,
                  pl.BlockSpec((tm, tn), lambda i, j: (i, j)),
                  pl.BlockSpec((2, tn), lambda i, j: (0, j))],
        out_specs=pl.BlockSpec((tm, tn), lambda i, j: (i, j)),
        scratch_shapes=[pltpu.VMEM((tm, d), BF16)],
        compiler_params=_params(("parallel", "arbitrary")),
        name="outproj",
    )(*ys, gb, w, x, gt)


def _ffn_norm_kernel(x_ref, g_ref, sh_ref, sc_ref, o_ref, gs_s, sh_s, *, n_lat):
    _norm_mod_rows(x_ref, g_ref, sh_ref, sc_ref, o_ref, gs_s, sh_s, pl.program_id(0) * x_ref.shape[0], n_lat)


def _ffn_norm(x, g, sh, sc, n_lat):
    r, d = x.shape
    tm = _pick(r, (640, 256))
    return pl.pallas_call(
        functools.partial(_ffn_norm_kernel, n_lat=n_lat),
        out_shape=jax.ShapeDtypeStruct((r, d), BF16),
        grid=(r // tm,),
        in_specs=[pl.BlockSpec((tm, d), lambda i: (i, 0)),
                  pl.BlockSpec((1, d), lambda i: (0, 0)),
                  pl.BlockSpec((2, d), lambda i: (0, 0)),
                  pl.BlockSpec((2, d), lambda i: (0, 0))],
        out_specs=pl.BlockSpec((tm, d), lambda i: (i, 0)),
        scratch_shapes=_norm_scratch(d),
        compiler_params=_params(("parallel",)),
        name="ffn_norm",
    )(x, g, sh, sc)


U32 = jnp.uint32
BF16_HI_MASK = 0xFFFF0000


def _pack_bf16_pairs(lo, hi):
    lo_bits = pltpu.bitcast(lo.astype(BF16).astype(F32), U32)
    hi_bits = pltpu.bitcast(hi.astype(BF16).astype(F32), U32)
    return hi_bits | (lo_bits >> 16)


def _unpack_bf16_pairs(packed):
    lo = pltpu.bitcast(packed << 16, F32).astype(BF16)
    hi = pltpu.bitcast(packed & U32(BF16_HI_MASK), F32).astype(BF16)
    return lo, hi


def _fit_tile(n, cands, vmem_bytes):
    for c in cands:
        if n % c == 0 and vmem_bytes(c) <= VMEM_BUDGET:
            return c
    raise ValueError(f"no tile in {cands} divides {n} within the VMEM budget")


def _sweep_blocks(lo, hi, loads, compute, stores):
    def start(copies):
        for cp in copies:
            cp.start()

    def wait(copies):
        for cp in copies:
            cp.wait()

    @pl.when(hi > lo)
    def _():
        start(loads(lo, 0))

        def body(b, carry):
            slot = (b - lo) & 1
            wait(loads(b, slot))

            @pl.when(b + 1 < hi)
            def _():
                start(loads(b + 1, 1 - slot))

            @pl.when(b - lo >= 2)
            def _():
                wait(stores(b - 2, slot))

            compute(b, slot)
            start(stores(b, slot))
            return carry

        lax.fori_loop(lo, hi, body, 0)

        @pl.when(hi - lo >= 2)
        def _():
            wait(stores(hi - 2, (hi - lo) & 1))

        wait(stores(hi - 1, (hi - 1 - lo) & 1))


def _zero_blocks(lo, hi, zbuf, store):
    @pl.when(hi > lo)
    def _():
        zbuf[...] = jnp.zeros_like(zbuf)

        def body(b, carry):
            cp = store(b)
            cp.start()
            cp.wait()
            return carry

        lax.fori_loop(lo, hi, body, 0)


def _up_kernel(lo_ref, hi_ref, nb_ref, x_hbm, w1_ref, w3_ref, g_hbm, xbuf, obuf, w1b, w3b, in_sem, out_sem):
    j = pl.program_id(0)
    e = pl.program_id(1)
    tb, tf = obuf.shape[1:]
    w1b[...] = w1_ref[...].astype(BF16)
    w3b[...] = w3_ref[...].astype(BF16)

    def rows(b):
        return pl.ds(pl.multiple_of(b * tb, tb), tb)

    def loads(b, slot):
        return (pltpu.make_async_copy(x_hbm.at[rows(b), :], xbuf.at[slot], in_sem.at[slot]),)

    def put(b, slot):
        return pltpu.make_async_copy(obuf.at[slot], g_hbm.at[rows(b), pl.ds(pl.multiple_of(j * tf, tf), tf)],
                                     out_sem.at[slot])

    def compute(b, slot):
        if xbuf.dtype == U32:
            x = jnp.concatenate(_unpack_bf16_pairs(xbuf[slot]), axis=1)
        else:
            x = xbuf[slot]
        a = jnp.dot(x, w1b[...], preferred_element_type=F32)
        c = jnp.dot(x, w3b[...], preferred_element_type=F32)
        obuf[slot] = (_silu(a) * c).astype(obuf.dtype)

    _sweep_blocks(lo_ref[e], hi_ref[e], loads, compute, lambda b, slot: (put(b, slot),))

    @pl.when(e == pl.num_programs(1) - 1)
    def _():
        _zero_blocks(hi_ref[e], nb_ref[0], obuf.at[0], lambda b: put(b, 0))


def _swiglu_up(xb, blk_lo, blk_hi, w1, w3, layer, tb):
    rows, xw = xb.shape
    n_exp, d, f = w1.shape[1:]
    wb, xbytes = w1.dtype.itemsize, xb.dtype.itemsize
    tf = _fit_tile(f, (1024, 512, 256, 128),
                   lambda t: 4 * d * t * wb + 4 * d * t + 2 * tb * xw * xbytes + 4 * tb * t + 2 * tb * d + 16 * tb * t)
    nb = jnp.full((1,), rows // tb, I32)
    w_spec = pl.BlockSpec((None, None, d, tf), lambda j, e, lo, hi, nb: (layer, e, 0, j))
    return pl.pallas_call(
        _up_kernel,
        out_shape=jax.ShapeDtypeStruct((rows, f), BF16),
        grid_spec=pltpu.PrefetchScalarGridSpec(
            num_scalar_prefetch=3,
            grid=(f // tf, n_exp),
            in_specs=[pl.BlockSpec(memory_space=pl.ANY), w_spec, w_spec],
            out_specs=pl.BlockSpec(memory_space=pl.ANY),
            scratch_shapes=[pltpu.VMEM((2, tb, xw), xb.dtype), pltpu.VMEM((2, tb, tf), BF16),
                            pltpu.VMEM((d, tf), BF16), pltpu.VMEM((d, tf), BF16),
                            pltpu.SemaphoreType.DMA((2,)), pltpu.SemaphoreType.DMA((2,))],
        ),
        compiler_params=_params(("arbitrary", "arbitrary")),
        name="swiglu_up",
    )(blk_lo, blk_hi, nb, xb, w1, w3)


def _down_kernel(lo_ref, hi_ref, nb_ref, g_hbm, w2_ref, *rest, n_lat):
    n = pl.program_id(0)
    e = pl.program_id(1)
    resid = len(rest) == 9
    if resid:
        x_hbm, gt_ref, y_hbm, gbuf, obuf, w2b, in_sem, out_sem, xbuf = rest
    else:
        y_hbm, gbuf, obuf, w2b, in_sem, out_sem = rest
    tb, tn = obuf.shape[1:]
    w2b[...] = w2_ref[...].astype(BF16)

    def rows(b):
        return pl.ds(pl.multiple_of(b * tb, tb), tb)

    cols = pl.ds(pl.multiple_of(n * tn, tn), tn)

    def loads(b, slot):
        cps = [pltpu.make_async_copy(g_hbm.at[rows(b), :], gbuf.at[slot], in_sem.at[0, slot])]
        if resid:
            cps.append(pltpu.make_async_copy(x_hbm.at[rows(b), cols], xbuf.at[slot], in_sem.at[1, slot]))
        return cps

    def put(b, slot):
        return pltpu.make_async_copy(obuf.at[slot], y_hbm.at[rows(b), cols], out_sem.at[slot])

    def compute(b, slot):
        y = jnp.dot(gbuf[slot], w2b[...], preferred_element_type=F32)
        if resid:
            y = xbuf[slot] + _row_select(gt_ref, b * tb, tb, n_lat) * y
        obuf[slot] = y

    _sweep_blocks(lo_ref[e], hi_ref[e], loads, compute, lambda b, slot: (put(b, slot),))

    @pl.when(e == pl.num_programs(1) - 1)
    def _():
        _zero_blocks(hi_ref[e], nb_ref[0], obuf.at[0], lambda b: put(b, 0))


def _swiglu_down(gact, blk_lo, blk_hi, w2, layer, tb, resid=None):
    rows, f = gact.shape
    n_exp, _, d = w2.shape[1:]
    wb = w2.dtype.itemsize
    tn = _fit_tile(d, (1024, 512, 256),
                   lambda t: 2 * f * t * wb + 2 * f * t + 4 * tb * f + 16 * tb * t + 4 * tb * t)
    nb = jnp.full((1,), rows // tb, I32)
    in_specs = [pl.BlockSpec(memory_space=pl.ANY),
                pl.BlockSpec((None, None, f, tn), lambda n, e, lo, hi, nb: (layer, e, 0, n))]
    args = [gact, w2]
    scratch = [pltpu.VMEM((2, tb, f), gact.dtype), pltpu.VMEM((2, tb, tn), F32), pltpu.VMEM((f, tn), BF16),
               pltpu.SemaphoreType.DMA((2, 2)), pltpu.SemaphoreType.DMA((2,))]
    n_lat = None
    if resid is not None:
        x, gt, n_lat = resid
        in_specs += [pl.BlockSpec(memory_space=pl.ANY),
                     pl.BlockSpec((2, tn), lambda n, e, lo, hi, nb: (0, n))]
        args += [x, gt]
        scratch.append(pltpu.VMEM((2, tb, tn), F32))
    return pl.pallas_call(
        functools.partial(_down_kernel, n_lat=n_lat),
        out_shape=jax.ShapeDtypeStruct((rows, d), F32),
        grid_spec=pltpu.PrefetchScalarGridSpec(
            num_scalar_prefetch=3,
            grid=(d // tn, n_exp),
            in_specs=in_specs,
            out_specs=pl.BlockSpec(memory_space=pl.ANY),
            scratch_shapes=scratch,
        ),
        compiler_params=_params(("arbitrary", "arbitrary")),
        name="swiglu_down",
    )(blk_lo, blk_hi, nb, *args)


def _dense_ffn(x, g2, sh, sc, gt, w1, w3, w2, layer, n_lat):
    r = x.shape[0]
    tb = _pick(r, (640, 256))
    blk_lo = jnp.zeros((1,), I32)
    blk_hi = jnp.full((1,), r // tb, I32)
    h = _ffn_norm(x, g2, sh, sc, n_lat)
    gact = _swiglu_up(h, blk_lo, blk_hi, w1[:, None], w3[:, None], layer, tb)
    return _swiglu_down(gact, blk_lo, blk_hi, w2[:, None], layer, tb, resid=(x, gt, n_lat))


def _route_kernel(x_ref, g_ref, sh_ref, sc_ref, wr_ref, br_ref, mi_ref, mf_ref, cnt_ref, h_s, run_s, gs_s, sh_s,
                  *, n_lat, n_exp):
    i = pl.program_id(0)
    tm = x_ref.shape[0]

    @pl.when(i == 0)
    def _():
        run_s[...] = jnp.zeros_like(run_s)

    _norm_mod_rows(x_ref, g_ref, sh_ref, sc_ref, h_s, gs_s, sh_s, i * tm, n_lat)
    h = h_s[...]
    w = wr_ref[...]
    h_hi = h.astype(BF16)
    h_lo = (h - h_hi.astype(F32)).astype(BF16)
    w_hi = w.astype(BF16)
    w_lo = (w - w_hi.astype(F32)).astype(BF16)
    logits = (jnp.dot(h_hi, w_hi, preferred_element_type=F32) + jnp.dot(h_hi, w_lo, preferred_element_type=F32)
              + jnp.dot(h_lo, w_hi, preferred_element_type=F32)) + br_ref[...]
    lane = lax.broadcasted_iota(I32, logits.shape, 1).astype(F32)
    l1 = jnp.where(lane < n_exp, logits, -jnp.inf)
    v1 = l1.max(axis=1, keepdims=True)
    e1 = jnp.where(l1 == v1, lane, float(V7X_LANES)).min(axis=1, keepdims=True)
    l2 = jnp.where(lane == e1, -jnp.inf, l1)
    v2 = l2.max(axis=1, keepdims=True)
    e2 = jnp.where(l2 == v2, lane, float(V7X_LANES)).min(axis=1, keepdims=True)
    t = jnp.exp(v2 - v1)
    g1 = 1.0 / (1.0 + t)
    g2 = t / (1.0 + t)

    onehot = jnp.where(jnp.logical_or(lane == e1, lane == e2), 1.0, 0.0)
    below = lax.broadcasted_iota(I32, (tm, tm), 0) > lax.broadcasted_iota(I32, (tm, tm), 1)
    before = jnp.dot(jnp.where(below, 1.0, 0.0).astype(BF16), onehot.astype(BF16),
                     preferred_element_type=F32) + run_s[...]
    r1 = jnp.where(lane == e1, before, 0.0).sum(axis=1, keepdims=True)
    r2 = jnp.where(lane == e2, before, 0.0).sum(axis=1, keepdims=True)
    run_s[...] = run_s[...] + onehot.sum(axis=0, keepdims=True)

    meta = jnp.where(lane == 0, e1, jnp.where(lane == 1, e2, jnp.where(lane == 2, r1, jnp.where(lane == 3, r2, 0.0))))
    mi_ref[...] = meta.astype(I32)
    mf_ref[...] = jnp.where(lane == 0, g1, jnp.where(lane == 1, g2, 0.0))
    cnt_ref[...] = jnp.broadcast_to(run_s[...], cnt_ref.shape)


def _route(x, g2, sh, sc, wr_pad, br_pad, n_lat, n_exp):
    r, d = x.shape
    tm = ROW_TILE
    return pl.pallas_call(
        functools.partial(_route_kernel, n_lat=n_lat, n_exp=n_exp),
        out_shape=(jax.ShapeDtypeStruct((r, V7X_LANES), I32),
                   jax.ShapeDtypeStruct((r, V7X_LANES), F32),
                   jax.ShapeDtypeStruct((V7X_SUBLANES, V7X_LANES), F32)),
        grid=(r // tm,),
        in_specs=[pl.BlockSpec((tm, d), lambda i: (i, 0)),
                  pl.BlockSpec((1, d), lambda i: (0, 0)),
                  pl.BlockSpec((2, d), lambda i: (0, 0)),
                  pl.BlockSpec((2, d), lambda i: (0, 0)),
                  pl.BlockSpec((d, V7X_LANES), lambda i: (0, 0)),
                  pl.BlockSpec((1, V7X_LANES), lambda i: (0, 0))],
        out_specs=(pl.BlockSpec((tm, V7X_LANES), lambda i: (i, 0)),
                   pl.BlockSpec((tm, V7X_LANES), lambda i: (i, 0)),
                   pl.BlockSpec((V7X_SUBLANES, V7X_LANES), lambda i: (0, 0))),
        scratch_shapes=[pltpu.VMEM((tm, d), F32), pltpu.VMEM((1, V7X_LANES), F32)] + _norm_scratch(d),
        compiler_params=_params(("arbitrary",)),
        name="route",
    )(x, g2, sh, sc, wr_pad, br_pad)


def _row_copy(src, s_row, dst, d_row, sem):
    return pltpu.make_async_copy(src.at[pl.ds(s_row, 1), :], dst.at[pl.ds(d_row, 1), :], sem)


def _dispatch_kernel(d1_ref, d2_ref, x_ref, g_ref, sh_ref, sc_ref, xb_in_ref, xb_ref, h_s, pk_s, sem, gs_s, sh_s,
                     *, n_lat):
    del xb_in_ref
    i = pl.program_id(0)
    tm, d = x_ref.shape
    _norm_mod_rows(x_ref, g_ref, sh_ref, sc_ref, h_s, gs_s, sh_s, i * tm, n_lat)
    pk_s[...] = _pack_bf16_pairs(h_s[:, :d // 2], h_s[:, d // 2:])

    def start(r, c):
        _row_copy(pk_s, r, xb_ref, d1_ref[0, r], sem.at[0]).start()
        _row_copy(pk_s, r, xb_ref, d2_ref[0, r], sem.at[1]).start()
        return c

    def wait(r, c):
        _row_copy(pk_s, r, xb_ref, d1_ref[0, r], sem.at[0]).wait()
        _row_copy(pk_s, r, xb_ref, d2_ref[0, r], sem.at[1]).wait()
        return c

    lax.fori_loop(0, tm, start, 0, unroll=ROW_DMA_UNROLL)
    lax.fori_loop(0, tm, wait, 0, unroll=ROW_DMA_UNROLL)


def _dispatch(x, g2, sh, sc, d1, d2, n_rows, n_lat):
    r, d = x.shape
    tm = ROW_TILE
    nt = r // tm
    smem_rows = pl.BlockSpec((None, 1, tm), lambda i: (i, 0, 0), memory_space=pltpu.SMEM)
    return pl.pallas_call(
        functools.partial(_dispatch_kernel, n_lat=n_lat),
        out_shape=jax.ShapeDtypeStruct((n_rows, d // 2), U32),
        grid=(nt,),
        in_specs=[smem_rows, smem_rows,
                  pl.BlockSpec((tm, d), lambda i: (i, 0)),
                  pl.BlockSpec((1, d), lambda i: (0, 0)),
                  pl.BlockSpec((2, d), lambda i: (0, 0)),
                  pl.BlockSpec((2, d), lambda i: (0, 0)),
                  pl.BlockSpec(memory_space=pl.ANY)],
        out_specs=pl.BlockSpec(memory_space=pl.ANY),
        scratch_shapes=[pltpu.VMEM((tm, d), F32), pltpu.VMEM((tm, d // 2), U32),
                        pltpu.SemaphoreType.DMA((2,))] + _norm_scratch(d),
        input_output_aliases={6: 0},
        compiler_params=_params(("arbitrary",)),
        name="moe_dispatch",
    )(d1.reshape(nt, 1, tm), d2.reshape(nt, 1, tm), x, g2, sh, sc, jnp.zeros((n_rows, d // 2), U32))


def _combine_kernel(d1_ref, d2_ref, mf_ref, x_ref, gt_ref, yb_ref, o_ref, buf, sem, *, n_lat):
    i = pl.program_id(0)
    tm = x_ref.shape[0]

    def start(r, c):
        _row_copy(yb_ref, d1_ref[0, r], buf.at[0], r, sem.at[0]).start()
        _row_copy(yb_ref, d2_ref[0, r], buf.at[1], r, sem.at[1]).start()
        return c

    def wait(r, c):
        _row_copy(yb_ref, d1_ref[0, r], buf.at[0], r, sem.at[0]).wait()
        _row_copy(yb_ref, d2_ref[0, r], buf.at[1], r, sem.at[1]).wait()
        return c

    lax.fori_loop(0, tm, start, 0, unroll=ROW_DMA_UNROLL)
    lax.fori_loop(0, tm, wait, 0, unroll=ROW_DMA_UNROLL)
    f = mf_ref[:, 0:1] * buf[0] + mf_ref[:, 1:2] * buf[1]
    o_ref[...] = x_ref[...] + _row_select(gt_ref, i * tm, tm, n_lat) * f


def _combine(x, gt, yb, d1, d2, mf, n_lat, n_out):
    r, d = x.shape
    tm = ROW_TILE
    nt = r // tm
    smem_rows = pl.BlockSpec((None, 1, tm), lambda i: (i, 0, 0), memory_space=pltpu.SMEM)
    return pl.pallas_call(
        functools.partial(_combine_kernel, n_lat=n_lat),
        out_shape=jax.ShapeDtypeStruct((n_out, d), F32),
        grid=(n_out // tm,),
        in_specs=[smem_rows, smem_rows,
                  pl.BlockSpec((tm, V7X_LANES), lambda i: (i, 0)),
                  pl.BlockSpec((tm, d), lambda i: (i, 0)),
                  pl.BlockSpec((2, d), lambda i: (0, 0)),
                  pl.BlockSpec(memory_space=pl.ANY)],
        out_specs=pl.BlockSpec((tm, d), lambda i: (i, 0)),
        scratch_shapes=[pltpu.VMEM((2, tm, d), F32), pltpu.SemaphoreType.DMA((2,))],
        compiler_params=_params(("arbitrary",)),
        name="moe_combine",
    )(d1.reshape(nt, 1, tm), d2.reshape(nt, 1, tm), mf, x, gt, yb)


def _moe_ffn(x, g2, sh, sc, gt, wr, br, w1, w3, w2, layer, n_lat, n_out):
    r, d = x.shape
    n_exp = wr.shape[1]
    blk = MOE_BLOCK
    wr_pad = jnp.zeros((d, V7X_LANES), F32).at[:, :n_exp].set(wr)
    br_pad = jnp.zeros((1, V7X_LANES), F32).at[0, :n_exp].set(br)
    mi, mf, cnt = _route(x, g2, sh, sc, wr_pad, br_pad, n_lat, n_exp)

    counts = cnt[0, :n_exp].astype(I32)
    padded = (counts + blk - 1) // blk * blk
    pad_end = jnp.cumsum(padded)
    pad_start = pad_end - padded
    d1 = pad_start[mi[:, 0]] + mi[:, 2]
    d2 = pad_start[mi[:, 1]] + mi[:, 3]
    n_rows = -(-(TOP_K * r + n_exp * (blk - 1)) // blk) * blk
    blk_lo = (pad_start // blk).astype(I32)
    blk_hi = (pad_end // blk).astype(I32)

    xb = _dispatch(x, g2, sh, sc, d1, d2, n_rows, n_lat)
    gact = _swiglu_up(xb, blk_lo, blk_hi, w1, w3, layer, blk)
    yb = _swiglu_down(gact, blk_lo, blk_hi, w2, layer, blk)
    return _combine(x, gt, yb, d1, d2, mf, n_lat, n_out)


def _rope_tables(n_lat, n_ctx):
    n_rows = n_lat // GRID_W
    axis_dim = HEAD_DIM // 2
    inv_freq = ROPE_THETA ** (-jnp.arange(0, axis_dim, 2, dtype=F32) / axis_dim)

    def axis_tables(n):
        ang = jnp.arange(n, dtype=F32)[:, None] * inv_freq[None, :]
        return jnp.cos(ang), jnp.sin(ang)

    per_row = lambda t: jnp.repeat(t, GRID_W, axis=0)
    per_col = lambda t: jnp.tile(t, (n_rows, 1))
    cr, sr = map(per_row, axis_tables(n_rows))
    cc, sc = map(per_col, axis_tables(GRID_W))
    cos_t = jnp.concatenate([cr, cr, cc, cc], axis=1)
    sin_t = jnp.concatenate([-sr, sr, -sc, sc], axis=1)
    cos_t = jnp.concatenate([cos_t, jnp.ones((n_ctx, HEAD_DIM), F32)], axis=0)
    sin_t = jnp.concatenate([sin_t, jnp.zeros((n_ctx, HEAD_DIM), F32)], axis=0)
    return cos_t, sin_t


def kernel(x, c, ctx, c_ctx, w_ada, b_ada, g_norm1, w_in, conv_w, conv_b, conv_ln_g, conv_ln_b, conv_pw, conv_pw_b, sgu_ln_g, sgu_ln_b, sgu_w, sgu_b, swa_q_g, swa_k_g, swa_sink, glb_q_g, glb_k_g, g_branch, w_out, g_norm2, ffn_w1, ffn_w3, ffn_w2, router_w, router_b, exp_w1, exp_w3, exp_w2):
    batch, n_lat, d = x.shape
    n_ctx = ctx.shape[1]
    depth = w_ada.shape[0]
    gw = d // N_GROUPS
    assert batch == 1 and n_lat % n_ctx == 0 and n_ctx % ROW_TILE == 0 and n_lat % GRID_W == 0
    assert conv_w.shape[1] // 2 < CONV_HALO and w_in.shape[2] == 4 * 2 * gw

    xs = jnp.concatenate([x[0], ctx[0]], axis=0)
    mods = _ada(jnp.stack([c[0], c_ctx], axis=1), w_ada, b_ada)
    cos_t, sin_t = _rope_tables(n_lat, n_ctx)
    row2 = lambda v: v.reshape(1, -1)

    for l in range(depth):
        sh1, sc1, gt1, sh2, sc2, gt2 = (mods[l, :, k * d:(k + 1) * d] for k in range(6))

        p = _inproj(xs, row2(g_norm1[l]), sh1, sc1, w_in[l].astype(BF16), n_lat)
        y_conv = _conv_group(p, conv_w[l], row2(conv_b[l]), row2(conv_ln_g[l]), row2(conv_ln_b[l]),
                             conv_pw[l].astype(BF16), row2(conv_pw_b[l]), n_lat)
        y_sgu = _sgu_group(p, row2(sgu_ln_g[l]), row2(sgu_ln_b[l]), sgu_w[l].astype(BF16), sgu_b[l].T)
        qt_s, k_s, vt_s, _ = _prep(p, row2(swa_q_g[l]), row2(swa_k_g[l]), cos_t, sin_t, block=2, feature_major=True)
        y_swa = _swa(qt_s, k_s, vt_s, swa_sink[l], n_lat)
        qt_g, k_g, vt_g, kn = _prep(p, row2(glb_q_g[l]), row2(glb_k_g[l]), cos_t, sin_t, block=3, feature_major=True)
        kmax = jnp.sqrt(jnp.max(kn[:, 0, :N_KV_HEADS], axis=0)) * KEY_NORM_MARGIN
        y_glb = _glb(qt_g, k_g, vt_g, kmax, n_lat)
        xs = _outproj((y_conv, y_sgu, y_swa, y_glb), row2(g_branch[l]), w_out[l].astype(BF16), xs, gt1, n_lat)

        if l % 2 == 0:
            xs = _dense_ffn(xs, row2(g_norm2[l]), sh2, sc2, gt2, ffn_w1, ffn_w3, ffn_w2, l // 2, n_lat)
        else:
            xs = _moe_ffn(xs, row2(g_norm2[l]), sh2, sc2, gt2, router_w[l // 2], router_b[l // 2],
                          exp_w1, exp_w3, exp_w2, l // 2, n_lat, n_out=n_lat if l == depth - 1 else n_lat + n_ctx)
    return xs[:n_lat][None]
```

```python
import functools
import math

import jax
import jax.numpy as jnp
from jax import lax
from jax.experimental import pallas as pl
from jax.experimental.pallas import tpu as pltpu

F32 = jnp.float32
BF16 = jnp.bfloat16
I32 = jnp.int32

HEAD_DIM = 128
N_GROUPS = 4
N_KV_HEADS = 2
GRID_W = 64
WINDOW = 128
ROPE_THETA = 10000.0
TOP_K = 2
MOE_BLOCK = 512
EPS = 1e-6
NEG = -1e30
SCALE = HEAD_DIM ** -0.5
Q_PRESCALE = SCALE * math.log2(math.e)
EXP2_SAFE_SHIFT = 60.0
KEY_NORM_MARGIN = 1.01

V7X_VMEM_BYTES = 64 * 1024 * 1024
V7X_LANES = 128
V7X_SUBLANES = 8
VMEM_BUDGET = V7X_VMEM_BYTES - 8 * 1024 * 1024

ROW_TILE = 256
CONV_HALO = 16
NORM_CHUNK = 16
NORM_UNROLL = 8
ROW_DMA_UNROLL = 8


def _pick(n, cands):
    for c in cands:
        if n % c == 0:
            return c
    raise ValueError(f"no tile in {cands} divides {n}")


def _params(sem, vmem=VMEM_BUDGET):
    return pltpu.CompilerParams(dimension_semantics=sem, vmem_limit_bytes=vmem)


def _sigmoid(x):
    return 1.0 / (1.0 + jnp.exp(-x))


def _silu(x):
    return x * _sigmoid(x)


def _gelu_tanh(x):
    c = math.sqrt(2.0 / math.pi)
    return 0.5 * x * (1.0 + jnp.tanh(c * (x + 0.044715 * (x * x * x))))


def _rms(x):
    return x * lax.rsqrt(jnp.mean(x * x, axis=-1, keepdims=True) + EPS)


def _layer_norm(x, g, b):
    xc = x - jnp.mean(x, axis=-1, keepdims=True)
    return xc * lax.rsqrt(jnp.mean(xc * xc, axis=-1, keepdims=True) + EPS) * g + b


def _row_select(vec2_ref, row0, tm, n_lat):
    rows = row0 + lax.broadcasted_iota(I32, (tm, 1), 0)
    return jnp.where(rows >= n_lat, vec2_ref[1:2, :], vec2_ref[0:1, :])


def _norm_mod(x, g, shift, scale):
    return _rms(x) * g * (1.0 + scale) + shift


def _norm_scratch(d):
    return [pltpu.VMEM((2 * V7X_SUBLANES, d), F32), pltpu.VMEM((2 * V7X_SUBLANES, d), F32)]


def _norm_mod_rows(x_ref, g_ref, sh_ref, sc_ref, out_ref, gs_s, sh_s, row0, n_lat):
    tm, d = x_ref.shape
    sub = V7X_SUBLANES
    for t in range(2):
        gs_s[t * sub:(t + 1) * sub, :] = jnp.broadcast_to(g_ref[...] * (1.0 + sc_ref[t:t + 1, :]), (sub, d))
        sh_s[t * sub:(t + 1) * sub, :] = jnp.broadcast_to(sh_ref[t:t + 1, :], (sub, d))

    def body(c, carry):
        r0 = pl.multiple_of(c * NORM_CHUNK, NORM_CHUNK)
        t0 = pl.multiple_of(jnp.where(row0 + r0 >= n_lat, sub, 0), sub)
        x = x_ref[pl.ds(r0, NORM_CHUNK), :].reshape(NORM_CHUNK // sub, sub, d)
        y = _rms(x) * gs_s[pl.ds(t0, sub), :] + sh_s[pl.ds(t0, sub), :]
        out_ref[pl.ds(r0, NORM_CHUNK), :] = y.reshape(NORM_CHUNK, d).astype(out_ref.dtype)
        return carry

    lax.fori_loop(0, tm // NORM_CHUNK, body, 0, unroll=NORM_UNROLL)


def _ada_kernel(s_ref, w_ref, b_ref, o_ref):
    d, tn = w_ref.shape
    kc = 64

    def body(c, acc):
        a0, a1 = acc
        k0 = pl.multiple_of(c * kc, kc)
        w = w_ref[pl.ds(k0, kc), :]
        s = _silu(s_ref[pl.ds(k0, kc), :])
        a0 = a0 + (w * s[:, 0:1]).reshape(kc // V7X_SUBLANES, V7X_SUBLANES, tn).sum(axis=0)
        a1 = a1 + (w * s[:, 1:2]).reshape(kc // V7X_SUBLANES, V7X_SUBLANES, tn).sum(axis=0)
        return a0, a1

    z = jnp.zeros((V7X_SUBLANES, tn), F32)
    a0, a1 = lax.fori_loop(0, d // kc, body, (z, z))
    o_ref[0:1, :] = a0.sum(axis=0, keepdims=True) + b_ref[...]
    o_ref[1:2, :] = a1.sum(axis=0, keepdims=True) + b_ref[...]


def _ada(cond, w_ada, b_ada):
    depth, d, n = w_ada.shape
    tn = _pick(n, (1024, 512, 256, 128))
    return pl.pallas_call(
        _ada_kernel,
        out_shape=jax.ShapeDtypeStruct((depth, 2, n), F32),
        grid=(depth, n // tn),
        in_specs=[
            pl.BlockSpec((d, 2), lambda l, j: (0, 0)),
            pl.BlockSpec((None, d, tn), lambda l, j: (l, 0, j)),
            pl.BlockSpec((None, 1, tn), lambda l, j: (l, 0, j)),
        ],
        out_specs=pl.BlockSpec((None, 2, tn), lambda l, j: (l, 0, j)),
        compiler_params=_params(("parallel", "parallel")),
        name="ada",
    )(cond, w_ada, b_ada.reshape(depth, 1, n))


def _weight_spec(k, n, itemsize, other_bytes):
    if other_bytes + k * n * itemsize <= VMEM_BUDGET:
        return n, pl.BlockSpec((k, n), lambda i, j: (0, 0), pipeline_mode=pl.Buffered(1))
    tn = _pick(n, (1024, 512, 256))
    return tn, pl.BlockSpec((k, tn), lambda i, j: (0, j))


def _inproj_kernel(x_ref, g_ref, sh_ref, sc_ref, w_ref, o_ref, h_s, gs_s, sh_s, *, n_lat):
    i = pl.program_id(0)
    tm = x_ref.shape[0]

    @pl.when(pl.program_id(1) == 0)
    def _():
        _norm_mod_rows(x_ref, g_ref, sh_ref, sc_ref, h_s, gs_s, sh_s, i * tm, n_lat)

    o_ref[...] = jnp.dot(h_s[...], w_ref[...], preferred_element_type=F32)


def _inproj(x, g, sh, sc, w, n_lat):
    r, d = x.shape
    n = w.shape[1]
    tm = _pick(r, (640, 256))
    tn, w_spec = _weight_spec(d, n, w.dtype.itemsize, 2 * tm * d * 4 + 2 * tm * n * 4 + tm * d * 2)
    return pl.pallas_call(
        functools.partial(_inproj_kernel, n_lat=n_lat),
        out_shape=jax.ShapeDtypeStruct((r, n), F32),
        grid=(r // tm, n // tn),
        in_specs=[
            pl.BlockSpec((tm, d), lambda i, j: (i, 0)),
            pl.BlockSpec((1, d), lambda i, j: (0, 0)),
            pl.BlockSpec((2, d), lambda i, j: (0, 0)),
            pl.BlockSpec((2, d), lambda i, j: (0, 0)),
            w_spec,
        ],
        out_specs=pl.BlockSpec((tm, tn), lambda i, j: (i, j)),
        scratch_shapes=[pltpu.VMEM((tm, d), BF16)] + _norm_scratch(d),
        compiler_params=_params(("parallel", "arbitrary")),
        name="inproj",
    )(x, g, sh, sc, w)


def _conv_kernel(pm_ref, pp_ref, pn_ref, cw_ref, cb_ref, lg_ref, lb_ref, pw_ref, pb_ref, o_ref, ext_s, sft_s,
                 *, n_lat_tiles, n_tiles):
    i = pl.program_id(0)
    tm, gw = o_ref.shape
    kw = cw_ref.shape[0]

    def glu(p):
        return p[:, :gw] * _sigmoid(p[:, gw:])

    prev_ok = jnp.logical_and(i != 0, i != n_lat_tiles)
    next_ok = jnp.logical_and(i != n_lat_tiles - 1, i != n_tiles - 1)
    ext_s[0:CONV_HALO, :] = jnp.where(prev_ok, glu(pp_ref[...]), 0.0)
    ext_s[CONV_HALO:CONV_HALO + tm, :] = glu(pm_ref[...])
    ext_s[CONV_HALO + tm:, :] = jnp.where(next_ok, glu(pn_ref[...]), 0.0)

    sub = V7X_SUBLANES
    span = sft_s.shape[1]
    for b in range(1, sub):
        sft_s[b - 1] = ext_s[b:b + span, :]
    acc = jnp.zeros((tm, gw), F32) + cb_ref[...]
    for k in range(kw):
        a, b = divmod(CONV_HALO - kw // 2 + k, sub)
        tap = ext_s[a * sub:a * sub + tm, :] if b == 0 else sft_s[b - 1, a * sub:a * sub + tm, :]
        acc = acc + cw_ref[k:k + 1, :] * tap
    y = _silu(_layer_norm(acc, lg_ref[...], lb_ref[...]))
    o_ref[...] = jnp.dot(y.astype(BF16), pw_ref[...], preferred_element_type=F32) + pb_ref[...]


def _conv_group(p, cw, cb, lg, lb, pw, pb, n_lat):
    r = p.shape[0]
    kw, gw = cw.shape
    tm = ROW_TILE
    n_tiles = r // tm
    hb = tm // CONV_HALO
    last_hb = r // CONV_HALO - 1
    return pl.pallas_call(
        functools.partial(_conv_kernel, n_lat_tiles=n_lat // tm, n_tiles=n_tiles),
        out_shape=jax.ShapeDtypeStruct((r, gw), F32),
        grid=(n_tiles,),
        in_specs=[
            pl.BlockSpec((tm, 2 * gw), lambda i: (i, 0)),
            pl.BlockSpec((CONV_HALO, 2 * gw), lambda i: (jnp.maximum(i * hb - 1, 0), 0)),
            pl.BlockSpec((CONV_HALO, 2 * gw), lambda i: (jnp.minimum((i + 1) * hb, last_hb), 0)),
            pl.BlockSpec((kw, gw), lambda i: (0, 0)),
            pl.BlockSpec((1, gw), lambda i: (0, 0)),
            pl.BlockSpec((1, gw), lambda i: (0, 0)),
            pl.BlockSpec((1, gw), lambda i: (0, 0)),
            pl.BlockSpec((gw, gw), lambda i: (0, 0)),
            pl.BlockSpec((1, gw), lambda i: (0, 0)),
        ],
        out_specs=pl.BlockSpec((tm, gw), lambda i: (i, 0)),
        scratch_shapes=[pltpu.VMEM((tm + 2 * CONV_HALO, gw), F32),
                        pltpu.VMEM((V7X_SUBLANES - 1, tm + 2 * CONV_HALO - V7X_SUBLANES, gw), F32)],
        compiler_params=_params(("parallel",)),
        name="conv_group",
    )(p, p, p, cw, cb, lg, lb, pw, pb)


def _sgu_kernel(p_ref, lg_ref, lb_ref, ws_ref, bs_ref, o_ref):
    tm, gw = o_ref.shape
    n_h, ch, _ = ws_ref.shape
    hd = gw // n_h
    z = _gelu_tanh(p_ref[...])
    u = z[:, :gw]
    v = _layer_norm(z[:, gw:], lg_ref[...], lb_ref[...]).astype(BF16)
    for c in range(tm // ch):
        rows = slice(c * ch, (c + 1) * ch)
        parts = []
        for h in range(n_h):
            s = jnp.dot(ws_ref[h], v[rows, h * hd:(h + 1) * hd], preferred_element_type=F32)
            parts.append(s + bs_ref[:, h:h + 1])
        o_ref[rows, :] = u[rows, :] * jnp.concatenate(parts, axis=1)


def _sgu_group(p, lg, lb, ws, bs_t):
    r = p.shape[0]
    gw = lg.shape[1]
    n_h, ch, _ = ws.shape
    tm = ROW_TILE
    return pl.pallas_call(
        _sgu_kernel,
        out_shape=jax.ShapeDtypeStruct((r, gw), F32),
        grid=(r // tm,),
        in_specs=[
            pl.BlockSpec((tm, 2 * gw), lambda i: (i, 1)),
            pl.BlockSpec((1, gw), lambda i: (0, 0)),
            pl.BlockSpec((1, gw), lambda i: (0, 0)),
            pl.BlockSpec((n_h, ch, ch), lambda i: (0, 0, 0)),
            pl.BlockSpec((ch, n_h), lambda i: (0, 0)),
        ],
        out_specs=pl.BlockSpec((tm, gw), lambda i: (i, 0)),
        compiler_params=_params(("parallel",)),
        name="sgu_group",
    )(p, lg, lb, ws, bs_t)


def _prep_kernel(p_ref, qg_ref, kg_ref, cos_ref, sin_ref, q_ref, k_ref, v_ref, *kn_ref, feature_major):
    tm = p_ref.shape[0]
    kvw = k_ref.shape[1]
    qw = p_ref.shape[1] - 2 * kvw
    cos = cos_ref[...]
    sin = sin_ref[...]
    lane = lax.broadcasted_iota(I32, (tm, HEAD_DIM), 1)
    low = (lane & (HEAD_DIM // 4)) == 0

    def head(x, g, mult):
        y = _rms(x) * g
        partner = jnp.where(low, pltpu.roll(y, HEAD_DIM - HEAD_DIM // 4, 1), pltpu.roll(y, HEAD_DIM // 4, 1))
        return (y * cos + partner * sin) * mult

    def put(ref, h, val):
        cols = slice(h * HEAD_DIM, (h + 1) * HEAD_DIM)
        if feature_major:
            ref[cols, :] = val.astype(BF16).T
        else:
            ref[:, cols] = val.astype(BF16)

    for h in range(qw // HEAD_DIM):
        put(q_ref, h, head(p_ref[:, h * HEAD_DIM:(h + 1) * HEAD_DIM], qg_ref[...], Q_PRESCALE))
    kn = None
    for h in range(kvw // HEAD_DIM):
        kh = head(p_ref[:, qw + h * HEAD_DIM:qw + (h + 1) * HEAD_DIM], kg_ref[...], 1.0)
        k_ref[:, h * HEAD_DIM:(h + 1) * HEAD_DIM] = kh.astype(BF16)
        put(v_ref, h, p_ref[:, qw + kvw + h * HEAD_DIM:qw + kvw + (h + 1) * HEAD_DIM])
        if feature_major:
            kn_lane = lax.broadcasted_iota(I32, kn_ref[0].shape, 1)
            ksq = jnp.sum(kh * kh, axis=1, keepdims=True).max(axis=0, keepdims=True)
            kn = jnp.where(kn_lane == h, ksq, jnp.zeros(kn_ref[0].shape, F32) if kn is None else kn)
    if feature_major:
        kn_ref[0][...] = kn


def _prep(p, qg, kg, cos_t, sin_t, block, feature_major):
    r = p.shape[0]
    qw = N_GROUPS * HEAD_DIM
    kvw = N_KV_HEADS * HEAD_DIM
    tm = ROW_TILE
    if feature_major:
        shape = lambda w: jax.ShapeDtypeStruct((w, r), BF16)
        spec = lambda w: pl.BlockSpec((w, tm), lambda i: (0, i))
    else:
        shape = lambda w: jax.ShapeDtypeStruct((r, w), BF16)
        spec = lambda w: pl.BlockSpec((tm, w), lambda i: (i, 0))
    out_shape = [shape(qw), jax.ShapeDtypeStruct((r, kvw), BF16), shape(kvw)]
    out_specs = [spec(qw), pl.BlockSpec((tm, kvw), lambda i: (i, 0)), spec(kvw)]
    if feature_major:
        out_shape.append(jax.ShapeDtypeStruct((r // tm, V7X_SUBLANES, V7X_LANES), F32))
        out_specs.append(pl.BlockSpec((None, V7X_SUBLANES, V7X_LANES), lambda i: (i, 0, 0)))
    return pl.pallas_call(
        functools.partial(_prep_kernel, feature_major=feature_major),
        out_shape=tuple(out_shape),
        grid=(r // tm,),
        in_specs=[
            pl.BlockSpec((tm, qw + 2 * kvw), lambda i: (i, block)),
            pl.BlockSpec((1, HEAD_DIM), lambda i: (0, 0)),
            pl.BlockSpec((1, HEAD_DIM), lambda i: (0, 0)),
            pl.BlockSpec((tm, HEAD_DIM), lambda i: (i, 0)),
            pl.BlockSpec((tm, HEAD_DIM), lambda i: (i, 0)),
        ],
        out_specs=tuple(out_specs),
        compiler_params=_params(("parallel",)),
        name="qkv_prep_t" if feature_major else "qkv_prep",
    )(p, qg, kg, cos_t, sin_t)


def _nt_dot(a, b):
    return lax.dot_general(a, b, (((1,), (1,)), ((), ())), preferred_element_type=F32)


def _swa_kernel(sink_ref, qt_ref, km_ref, kp_ref, kn_ref, vm_ref, vp_ref, vn_ref, kc_ref, vc_ref, o_ref,
                *, n_lat):
    i = pl.program_id(0)
    tq = qt_ref.shape[1]
    hb = kp_ref.shape[0]
    n_rep = (qt_ref.shape[0] // HEAD_DIM) // N_KV_HEADS
    nk = tq + 2 * hb
    kj = lax.broadcasted_iota(I32, (nk, tq), 0) - hb
    qi = lax.broadcasted_iota(I32, (nk, tq), 1)
    kglob = i * tq + kj
    ok1 = (jnp.abs(kj - qi) <= WINDOW) & (kglob >= 0) & (kglob < n_lat) & (i * tq < n_lat)
    ok = jnp.concatenate([ok1.astype(F32)] * n_rep, axis=1) > 0.5
    q_lane = lax.broadcasted_iota(I32, (1, n_rep * tq), 1)
    for j in range(N_KV_HEADS):
        kv = slice(j * HEAD_DIM, (j + 1) * HEAD_DIM)
        q2t = jnp.concatenate(
            [qt_ref[(j * n_rep + g) * HEAD_DIM:(j * n_rep + g + 1) * HEAD_DIM, :] for g in range(n_rep)], axis=1)
        kw = jnp.concatenate([kp_ref[:, kv], km_ref[:, kv], kn_ref[:, kv]], axis=0)
        vwt = jnp.concatenate([vp_ref[kv, :], vm_ref[kv, :], vn_ref[kv, :]], axis=1)
        s_w = jnp.where(ok, jnp.dot(kw, q2t, preferred_element_type=F32), NEG)
        s_c = jnp.dot(kc_ref[:, kv], q2t, preferred_element_type=F32)
        sk = jnp.zeros((1, n_rep * tq), F32)
        for g in range(n_rep):
            sk = jnp.where(q_lane >= g * tq, sink_ref[j * n_rep + g] * math.log2(math.e), sk)
        m = jnp.maximum(jnp.maximum(s_w.max(axis=0, keepdims=True), s_c.max(axis=0, keepdims=True)), sk)
        p_w = jnp.exp2(s_w - m)
        p_c = jnp.exp2(s_c - m)
        den = jnp.exp2(sk - m) + p_w.sum(axis=0, keepdims=True) + p_c.sum(axis=0, keepdims=True)
        ot = (jnp.dot(vwt, p_w.astype(BF16), preferred_element_type=F32)
              + jnp.dot(vc_ref[kv, :], p_c.astype(BF16), preferred_element_type=F32)) / den
        o = ot.T
        for g in range(n_rep):
            h = j * n_rep + g
            o_ref[:, h * HEAD_DIM:(h + 1) * HEAD_DIM] = o[g * tq:(g + 1) * tq, :]


def _swa(qt, k, vt, sink, n_lat):
    qw = qt.shape[0]
    r, kvw = k.shape
    n_ctx = r - n_lat
    tq = ROW_TILE
    hb = WINDOW
    per = tq // hb
    last_hb = r // hb - 1
    prev_b = lambda i: jnp.maximum(i * per - 1, 0)
    next_b = lambda i: jnp.minimum((i + 1) * per, last_hb)
    return pl.pallas_call(
        functools.partial(_swa_kernel, n_lat=n_lat),
        out_shape=jax.ShapeDtypeStruct((r, qw), F32),
        grid=(r // tq,),
        in_specs=[
            pl.BlockSpec(memory_space=pltpu.SMEM),
            pl.BlockSpec((qw, tq), lambda i: (0, i)),
            pl.BlockSpec((tq, kvw), lambda i: (i, 0)),
            pl.BlockSpec((hb, kvw), lambda i: (prev_b(i), 0)),
            pl.BlockSpec((hb, kvw), lambda i: (next_b(i), 0)),
            pl.BlockSpec((kvw, tq), lambda i: (0, i)),
            pl.BlockSpec((kvw, hb), lambda i: (0, prev_b(i))),
            pl.BlockSpec((kvw, hb), lambda i: (0, next_b(i))),
            pl.BlockSpec((n_ctx, kvw), lambda i: (n_lat // n_ctx, 0)),
            pl.BlockSpec((kvw, n_ctx), lambda i: (0, n_lat // n_ctx)),
        ],
        out_specs=pl.BlockSpec((tq, qw), lambda i: (i, 0)),
        compiler_params=_params(("parallel",)),
        name="window_attn",
    )(sink, qt, k, k, k, vt, vt, vt, k, vt)


def _glb_kernel(kmax_ref, qt_ref, k_ref, vt_ref, o_ref, acc_s, l_s, *, n_lat, tk, unroll):
    j = pl.program_id(0)
    i = pl.program_id(1)
    tq = qt_ref.shape[1]
    n_rep = qt_ref.shape[0] // HEAD_DIM
    nq = n_rep * tq
    n_ctx = k_ref.shape[0] - n_lat
    q2t = jnp.concatenate([qt_ref[g * HEAD_DIM:(g + 1) * HEAD_DIM, :] for g in range(n_rep)], axis=1)
    qf = q2t.astype(F32)
    bound = jnp.sqrt(jnp.sum(qf * qf, axis=0, keepdims=True)) * kmax_ref[j]

    def scores(k0, size):
        return jnp.dot(k_ref[pl.ds(k0, size), :], q2t, preferred_element_type=F32)

    def shifted_by_bound():
        def add(k0, size):
            p = jnp.exp2(scores(k0, size) - bound)
            l_s[...] += p.sum(axis=0, keepdims=True)
            acc_s[...] += jnp.dot(vt_ref[:, pl.ds(k0, size)], p.astype(BF16), preferred_element_type=F32)

        acc_s[...] = jnp.zeros_like(acc_s)
        l_s[...] = jnp.zeros_like(l_s)
        add(n_lat, n_ctx)

        @pl.when(i * tq < n_lat)
        def _():
            def body(c, carry):
                for u in range(unroll):
                    add(pl.multiple_of((c * unroll + u) * tk, tk), tk)
                return carry

            lax.fori_loop(0, n_lat // tk // unroll, body, 0)

    def shifted_by_running_max():
        def update(s, vt, m, l):
            m_new = jnp.maximum(m, s.max(axis=0, keepdims=True))
            alpha = jnp.exp2(m - m_new)
            p = jnp.exp2(s - m_new)
            l = alpha * l + p.sum(axis=0, keepdims=True)
            acc_s[...] = alpha * acc_s[...] + jnp.dot(vt, p.astype(BF16), preferred_element_type=F32)
            return m_new, l

        acc_s[...] = jnp.zeros_like(acc_s)
        first = update(scores(n_lat, n_ctx), vt_ref[:, n_lat:],
                       jnp.full((1, nq), NEG, F32), jnp.zeros((1, nq), F32))

        def body(c, carry):
            k0 = pl.multiple_of(c * tk, tk)
            return update(scores(k0, tk), vt_ref[:, pl.ds(k0, tk)], *carry)

        _, l = lax.fori_loop(0, jnp.where(i * tq < n_lat, n_lat // tk, 0), body, first)
        l_s[...] = l

    lax.cond(jnp.max(bound) <= EXP2_SAFE_SHIFT, shifted_by_bound, shifted_by_running_max)
    o = (acc_s[...] / l_s[...]).T
    for g in range(n_rep):
        o_ref[:, g * HEAD_DIM:(g + 1) * HEAD_DIM] = o[g * tq:(g + 1) * tq, :]


def _glb(qt, k, vt, kmax, n_lat):
    r, kvw = k.shape
    qw = qt.shape[0]
    tq = ROW_TILE
    gq = qw // N_KV_HEADS
    tk = _pick(n_lat, (2048, 1024, 512, 256))
    unroll = 2 if (n_lat // tk) % 2 == 0 else 1
    nq = gq // HEAD_DIM * tq
    return pl.pallas_call(
        functools.partial(_glb_kernel, n_lat=n_lat, tk=tk, unroll=unroll),
        out_shape=jax.ShapeDtypeStruct((r, qw), F32),
        grid=(N_KV_HEADS, r // tq),
        in_specs=[
            pl.BlockSpec(memory_space=pltpu.SMEM),
            pl.BlockSpec((gq, tq), lambda j, i: (j, i)),
            pl.BlockSpec((r, HEAD_DIM), lambda j, i: (0, j)),
            pl.BlockSpec((HEAD_DIM, r), lambda j, i: (j, 0)),
        ],
        out_specs=pl.BlockSpec((tq, gq), lambda j, i: (i, j)),
        scratch_shapes=[pltpu.VMEM((HEAD_DIM, nq), F32), pltpu.VMEM((1, nq), F32)],
        compiler_params=_params(("parallel", "parallel")),
        name="global_attn",
    )(kmax, qt, k, vt)


def _outproj_kernel(y0_ref, y1_ref, y2_ref, y3_ref, gb_ref, w_ref, x_ref, gt_ref, o_ref, yn_s, *, n_lat):
    i = pl.program_id(0)
    tm = x_ref.shape[0]

    @pl.when(pl.program_id(1) == 0)
    def _():
        for g, y_ref in enumerate((y0_ref, y1_ref, y2_ref, y3_ref)):
            gw = y_ref.shape[1]
            cols = slice(g * gw, (g + 1) * gw)
            yn_s[:, cols] = (_rms(y_ref[...]) * gb_ref[:, cols]).astype(BF16)

    gate = _row_select(gt_ref, i * tm, tm, n_lat)
    o_ref[...] = x_ref[...] + gate * jnp.dot(yn_s[...], w_ref[...], preferred_element_type=F32)


def _outproj(ys, gb, w, x, gt, n_lat):
    r, d = x.shape
    gw = ys[0].shape[1]
    tm = _pick(r, (640, 256))
    tn, w_spec = _weight_spec(d, d, w.dtype.itemsize, 2 * tm * d * 4 * 3 + tm * d * 2)
    ysp = pl.BlockSpec((tm, gw), lambda i, j: (i, 0))
    return pl.pallas_call(
        functools.partial(_outproj_kernel, n_lat=n_lat),
        out_shape=jax.ShapeDtypeStruct((r, d), F32),
        grid=(r // tm, d // tn),
        in_specs=[ysp, ysp, ysp, ysp,
                  pl.BlockSpec((1, d), lambda i, j: (0, 0)),
                  w_spec,
                  pl.BlockSpec((tm, tn), lambda i, j: (i, j)),
                  pl.BlockSpec((2, tn), lambda i, j: (0, j))],
        out_specs=pl.BlockSpec((tm, tn), lambda i, j: (i, j)),
        scratch_shapes=[pltpu.VMEM((tm, d), BF16)],
        compiler_params=_params(("parallel", "arbitrary")),
        name="outproj",
    )(*ys, gb, w, x, gt)


def _ffn_norm_kernel(x_ref, g_ref, sh_ref, sc_ref, o_ref, gs_s, sh_s, *, n_lat):
    _norm_mod_rows(x_ref, g_ref, sh_ref, sc_ref, o_ref, gs_s, sh_s, pl.program_id(0) * x_ref.shape[0], n_lat)


def _ffn_norm(x, g, sh, sc, n_lat):
    r, d = x.shape
    tm = _pick(r, (640, 256))
    return pl.pallas_call(
        functools.partial(_ffn_norm_kernel, n_lat=n_lat),
        out_shape=jax.ShapeDtypeStruct((r, d), BF16),
        grid=(r // tm,),
        in_specs=[pl.BlockSpec((tm, d), lambda i: (i, 0)),
                  pl.BlockSpec((1, d), lambda i: (0, 0)),
                  pl.BlockSpec((2, d), lambda i: (0, 0)),
                  pl.BlockSpec((2, d), lambda i: (0, 0))],
        out_specs=pl.BlockSpec((tm, d), lambda i: (i, 0)),
        scratch_shapes=_norm_scratch(d),
        compiler_params=_params(("parallel",)),
        name="ffn_norm",
    )(x, g, sh, sc)


U32 = jnp.uint32
BF16_HI_MASK = 0xFFFF0000


def _pack_bf16_pairs(lo, hi):
    lo_bits = pltpu.bitcast(lo.astype(BF16).astype(F32), U32)
    hi_bits = pltpu.bitcast(hi.astype(BF16).astype(F32), U32)
    return hi_bits | (lo_bits >> 16)


def _unpack_bf16_pairs(packed):
    lo = pltpu.bitcast(packed << 16, F32).astype(BF16)
    hi = pltpu.bitcast(packed & U32(BF16_HI_MASK), F32).astype(BF16)
    return lo, hi


def _fit_tile(n, cands, vmem_bytes):
    for c in cands:
        if n % c == 0 and vmem_bytes(c) <= VMEM_BUDGET:
            return c
    raise ValueError(f"no tile in {cands} divides {n} within the VMEM budget")


def _sweep_blocks(lo, hi, n_live, p, n_pass, loads, compute, stores):
    def start(copies):
        for cp in copies:
            cp.start()

    def wait(copies):
        for cp in copies:
            cp.wait()

    t_end = n_pass * n_live

    @pl.when(jnp.logical_and(p == 0, jnp.logical_and(lo == 0, hi > 0)))
    def _():
        start(loads(0, 0, 0))

    def body(b, carry):
        t = p * n_live + b
        slot = t & 1
        wait(loads(b, p, slot))
        wrap = b + 1 >= n_live

        @pl.when(t + 1 < t_end)
        def _():
            start(loads(jnp.where(wrap, 0, b + 1), jnp.where(wrap, p + 1, p), 1 - slot))

        @pl.when(t >= 2)
        def _():
            wait(stores(b, p, slot))
        compute(b, slot)
        start(stores(b, p, slot))
        return carry

    lax.fori_loop(lo, hi, body, 0)

    @pl.when(jnp.logical_and(hi > lo, p * n_live + hi == t_end))
    def _():
        @pl.when(t_end >= 2)
        def _():
            wait(stores(hi - 1, p, t_end & 1))
        wait(stores(hi - 1, p, (t_end - 1) & 1))


def _zero_blocks(lo, hi, zbuf, store):
    @pl.when(hi > lo)
    def _():
        zbuf[...] = jnp.zeros_like(zbuf)

        def body(b, carry):
            cp = store(b)
            cp.start()
            cp.wait()
            return carry

        lax.fori_loop(lo, hi, body, 0)


def _up_kernel(lo_ref, hi_ref, nb_ref, x_hbm, w1_ref, w3_ref, g_hbm, xbuf, obuf, zbuf, w1b, w3b, in_sem, out_sem, z_sem):
    j = pl.program_id(0)
    e = pl.program_id(1)
    last_e = pl.num_programs(1) - 1
    tb, tf = obuf.shape[1:]
    w1b[...] = w1_ref[...].astype(BF16)
    w3b[...] = w3_ref[...].astype(BF16)

    def rows(b):
        return pl.ds(pl.multiple_of(b * tb, tb), tb)

    def cols(p):
        return pl.ds(pl.multiple_of(p * tf, tf), tf)

    def loads(b, p, slot):
        return (pltpu.make_async_copy(x_hbm.at[rows(b), :], xbuf.at[slot], in_sem.at[slot]),)

    def stores(b, p, slot):
        return (pltpu.make_async_copy(obuf.at[slot], g_hbm.at[rows(b), cols(p)], out_sem.at[slot]),)

    def compute(b, slot):
        if xbuf.dtype == U32:
            x = jnp.concatenate(_unpack_bf16_pairs(xbuf[slot]), axis=1)
        else:
            x = xbuf[slot]
        a = jnp.dot(x, w1b[...], preferred_element_type=F32)
        c = jnp.dot(x, w3b[...], preferred_element_type=F32)
        obuf[slot] = (_silu(a) * c).astype(obuf.dtype)

    _sweep_blocks(lo_ref[e], hi_ref[e], hi_ref[last_e], j, pl.num_programs(0), loads, compute, stores)

    @pl.when(e == last_e)
    def _():
        _zero_blocks(hi_ref[e], nb_ref[0], zbuf,
                     lambda b: pltpu.make_async_copy(zbuf, g_hbm.at[rows(b), cols(j)], z_sem.at[0]))


def _swiglu_up(xb, blk_lo, blk_hi, w1, w3, layer, tb):
    rows, xw = xb.shape
    n_exp, d, f = w1.shape[1:]
    wb, xbytes = w1.dtype.itemsize, xb.dtype.itemsize
    tf = _fit_tile(f, (1024, 512, 256, 128),
                   lambda t: 4 * d * t * wb + 4 * d * t + 2 * tb * xw * xbytes + 4 * tb * t + 2 * tb * d + 16 * tb * t)
    nb = jnp.full((1,), rows // tb, I32)
    w_spec = pl.BlockSpec((None, None, d, tf), lambda j, e, lo, hi, nb: (layer, e, 0, j))
    return pl.pallas_call(
        _up_kernel,
        out_shape=jax.ShapeDtypeStruct((rows, f), BF16),
        grid_spec=pltpu.PrefetchScalarGridSpec(
            num_scalar_prefetch=3,
            grid=(f // tf, n_exp),
            in_specs=[pl.BlockSpec(memory_space=pl.ANY), w_spec, w_spec],
            out_specs=pl.BlockSpec(memory_space=pl.ANY),
            scratch_shapes=[pltpu.VMEM((2, tb, xw), xb.dtype), pltpu.VMEM((2, tb, tf), BF16),
                            pltpu.VMEM((tb, tf), BF16), pltpu.VMEM((d, tf), BF16), pltpu.VMEM((d, tf), BF16),
                            pltpu.SemaphoreType.DMA((2,)), pltpu.SemaphoreType.DMA((2,)),
                            pltpu.SemaphoreType.DMA((1,))],
        ),
        compiler_params=_params(("arbitrary", "arbitrary")),
        name="swiglu_up",
    )(blk_lo, blk_hi, nb, xb, w1, w3)


def _down_kernel(lo_ref, hi_ref, nb_ref, g_hbm, w2_ref, *rest, n_lat):
    n = pl.program_id(0)
    e = pl.program_id(1)
    last_e = pl.num_programs(1) - 1
    resid = len(rest) == 11
    if resid:
        x_hbm, gt_ref, y_hbm, gbuf, obuf, zbuf, w2b, in_sem, out_sem, z_sem, xbuf = rest
    else:
        y_hbm, gbuf, obuf, zbuf, w2b, in_sem, out_sem, z_sem = rest
    tb, tn = obuf.shape[1:]
    w2b[...] = w2_ref[...].astype(BF16)

    def rows(b):
        return pl.ds(pl.multiple_of(b * tb, tb), tb)

    def cols(p):
        return pl.ds(pl.multiple_of(p * tn, tn), tn)

    def loads(b, p, slot):
        cps = [pltpu.make_async_copy(g_hbm.at[rows(b), :], gbuf.at[slot], in_sem.at[0, slot])]
        if resid:
            cps.append(pltpu.make_async_copy(x_hbm.at[rows(b), cols(p)], xbuf.at[slot], in_sem.at[1, slot]))
        return cps

    def stores(b, p, slot):
        return (pltpu.make_async_copy(obuf.at[slot], y_hbm.at[rows(b), cols(p)], out_sem.at[slot]),)

    def compute(b, slot):
        y = jnp.dot(gbuf[slot], w2b[...], preferred_element_type=F32)
        if resid:
            y = xbuf[slot] + _row_select(gt_ref, b * tb, tb, n_lat) * y
        obuf[slot] = y

    _sweep_blocks(lo_ref[e], hi_ref[e], hi_ref[last_e], n, pl.num_programs(0), loads, compute, stores)

    @pl.when(e == last_e)
    def _():
        _zero_blocks(hi_ref[e], nb_ref[0], zbuf,
                     lambda b: pltpu.make_async_copy(zbuf, y_hbm.at[rows(b), cols(n)], z_sem.at[0]))


def _swiglu_down(gact, blk_lo, blk_hi, w2, layer, tb, resid=None):
    rows, f = gact.shape
    n_exp, _, d = w2.shape[1:]
    wb = w2.dtype.itemsize
    tn = _fit_tile(d, (1024, 512, 256),
                   lambda t: 2 * f * t * wb + 2 * f * t + 4 * tb * f + 16 * tb * t + 4 * tb * t)
    nb = jnp.full((1,), rows // tb, I32)
    in_specs = [pl.BlockSpec(memory_space=pl.ANY),
                pl.BlockSpec((None, None, f, tn), lambda n, e, lo, hi, nb: (layer, e, 0, n))]
    args = [gact, w2]
    scratch = [pltpu.VMEM((2, tb, f), gact.dtype), pltpu.VMEM((2, tb, tn), F32), pltpu.VMEM((tb, tn), F32),
               pltpu.VMEM((f, tn), BF16),
               pltpu.SemaphoreType.DMA((2, 2)), pltpu.SemaphoreType.DMA((2,)), pltpu.SemaphoreType.DMA((1,))]
    n_lat = None
    if resid is not None:
        x, gt, n_lat = resid
        in_specs += [pl.BlockSpec(memory_space=pl.ANY),
                     pl.BlockSpec((2, tn), lambda n, e, lo, hi, nb: (0, n))]
        args += [x, gt]
        scratch.append(pltpu.VMEM((2, tb, tn), F32))
    return pl.pallas_call(
        functools.partial(_down_kernel, n_lat=n_lat),
        out_shape=jax.ShapeDtypeStruct((rows, d), F32),
        grid_spec=pltpu.PrefetchScalarGridSpec(
            num_scalar_prefetch=3,
            grid=(d // tn, n_exp),
            in_specs=in_specs,
            out_specs=pl.BlockSpec(memory_space=pl.ANY),
            scratch_shapes=scratch,
        ),
        compiler_params=_params(("arbitrary", "arbitrary")),
        name="swiglu_down",
    )(blk_lo, blk_hi, nb, *args)


def _dense_ffn(x, g2, sh, sc, gt, w1, w3, w2, layer, n_lat):
    r = x.shape[0]
    tb = _pick(r, (640, 256))
    blk_lo = jnp.zeros((1,), I32)
    blk_hi = jnp.full((1,), r // tb, I32)
    h = _ffn_norm(x, g2, sh, sc, n_lat)
    gact = _swiglu_up(h, blk_lo, blk_hi, w1[:, None], w3[:, None], layer, tb)
    return _swiglu_down(gact, blk_lo, blk_hi, w2[:, None], layer, tb, resid=(x, gt, n_lat))


def _route_kernel(x_ref, g_ref, sh_ref, sc_ref, wr_ref, br_ref, mi_ref, mf_ref, cnt_ref, h_s, run_s, gs_s, sh_s,
                  *, n_lat, n_exp):
    i = pl.program_id(0)
    tm = x_ref.shape[0]

    @pl.when(i == 0)
    def _():
        run_s[...] = jnp.zeros_like(run_s)

    _norm_mod_rows(x_ref, g_ref, sh_ref, sc_ref, h_s, gs_s, sh_s, i * tm, n_lat)
    h = h_s[...]
    w = wr_ref[...]
    h_hi = h.astype(BF16)
    h_lo = (h - h_hi.astype(F32)).astype(BF16)
    w_hi = w.astype(BF16)
    w_lo = (w - w_hi.astype(F32)).astype(BF16)
    logits = (jnp.dot(h_hi, w_hi, preferred_element_type=F32) + jnp.dot(h_hi, w_lo, preferred_element_type=F32)
              + jnp.dot(h_lo, w_hi, preferred_element_type=F32)) + br_ref[...]
    lane = lax.broadcasted_iota(I32, logits.shape, 1).astype(F32)
    l1 = jnp.where(lane < n_exp, logits, -jnp.inf)
    v1 = l1.max(axis=1, keepdims=True)
    e1 = jnp.where(l1 == v1, lane, float(V7X_LANES)).min(axis=1, keepdims=True)
    l2 = jnp.where(lane == e1, -jnp.inf, l1)
    v2 = l2.max(axis=1, keepdims=True)
    e2 = jnp.where(l2 == v2, lane, float(V7X_LANES)).min(axis=1, keepdims=True)
    t = jnp.exp(v2 - v1)
    g1 = 1.0 / (1.0 + t)
    g2 = t / (1.0 + t)

    onehot = jnp.where(jnp.logical_or(lane == e1, lane == e2), 1.0, 0.0)
    below = lax.broadcasted_iota(I32, (tm, tm), 0) > lax.broadcasted_iota(I32, (tm, tm), 1)
    before = jnp.dot(jnp.where(below, 1.0, 0.0).astype(BF16), onehot.astype(BF16),
                     preferred_element_type=F32) + run_s[...]
    r1 = jnp.where(lane == e1, before, 0.0).sum(axis=1, keepdims=True)
    r2 = jnp.where(lane == e2, before, 0.0).sum(axis=1, keepdims=True)
    run_s[...] = run_s[...] + onehot.sum(axis=0, keepdims=True)

    meta = jnp.where(lane == 0, e1, jnp.where(lane == 1, e2, jnp.where(lane == 2, r1, jnp.where(lane == 3, r2, 0.0))))
    mi_ref[...] = meta.astype(I32)
    mf_ref[...] = jnp.where(lane == 0, g1, jnp.where(lane == 1, g2, 0.0))
    cnt_ref[...] = jnp.broadcast_to(run_s[...], cnt_ref.shape)


def _route(x, g2, sh, sc, wr_pad, br_pad, n_lat, n_exp):
    r, d = x.shape
    tm = ROW_TILE
    return pl.pallas_call(
        functools.partial(_route_kernel, n_lat=n_lat, n_exp=n_exp),
        out_shape=(jax.ShapeDtypeStruct((r, V7X_LANES), I32),
                   jax.ShapeDtypeStruct((r, V7X_LANES), F32),
                   jax.ShapeDtypeStruct((V7X_SUBLANES, V7X_LANES), F32)),
        grid=(r // tm,),
        in_specs=[pl.BlockSpec((tm, d), lambda i: (i, 0)),
                  pl.BlockSpec((1, d), lambda i: (0, 0)),
                  pl.BlockSpec((2, d), lambda i: (0, 0)),
                  pl.BlockSpec((2, d), lambda i: (0, 0)),
                  pl.BlockSpec((d, V7X_LANES), lambda i: (0, 0)),
                  pl.BlockSpec((1, V7X_LANES), lambda i: (0, 0))],
        out_specs=(pl.BlockSpec((tm, V7X_LANES), lambda i: (i, 0)),
                   pl.BlockSpec((tm, V7X_LANES), lambda i: (i, 0)),
                   pl.BlockSpec((V7X_SUBLANES, V7X_LANES), lambda i: (0, 0))),
        scratch_shapes=[pltpu.VMEM((tm, d), F32), pltpu.VMEM((1, V7X_LANES), F32)] + _norm_scratch(d),
        compiler_params=_params(("arbitrary",)),
        name="route",
    )(x, g2, sh, sc, wr_pad, br_pad)


def _row_copy(src, s_row, dst, d_row, sem):
    return pltpu.make_async_copy(src.at[pl.ds(s_row, 1), :], dst.at[pl.ds(d_row, 1), :], sem)


def _dispatch_kernel(d1_ref, d2_ref, x_ref, g_ref, sh_ref, sc_ref, xb_in_ref, xb_ref, h_s, pk_s, sem, gs_s, sh_s,
                     *, n_lat):
    del xb_in_ref
    i = pl.program_id(0)
    tm, d = x_ref.shape
    _norm_mod_rows(x_ref, g_ref, sh_ref, sc_ref, h_s, gs_s, sh_s, i * tm, n_lat)
    pk_s[...] = _pack_bf16_pairs(h_s[:, :d // 2], h_s[:, d // 2:])

    def start(r, c):
        _row_copy(pk_s, r, xb_ref, d1_ref[0, r], sem.at[0]).start()
        _row_copy(pk_s, r, xb_ref, d2_ref[0, r], sem.at[1]).start()
        return c

    def wait(r, c):
        _row_copy(pk_s, r, xb_ref, d1_ref[0, r], sem.at[0]).wait()
        _row_copy(pk_s, r, xb_ref, d2_ref[0, r], sem.at[1]).wait()
        return c

    lax.fori_loop(0, tm, start, 0, unroll=ROW_DMA_UNROLL)
    lax.fori_loop(0, tm, wait, 0, unroll=ROW_DMA_UNROLL)


def _dispatch(x, g2, sh, sc, d1, d2, n_rows, n_lat):
    r, d = x.shape
    tm = ROW_TILE
    nt = r // tm
    smem_rows = pl.BlockSpec((None, 1, tm), lambda i: (i, 0, 0), memory_space=pltpu.SMEM)
    return pl.pallas_call(
        functools.partial(_dispatch_kernel, n_lat=n_lat),
        out_shape=jax.ShapeDtypeStruct((n_rows, d // 2), U32),
        grid=(nt,),
        in_specs=[smem_rows, smem_rows,
                  pl.BlockSpec((tm, d), lambda i: (i, 0)),
                  pl.BlockSpec((1, d), lambda i: (0, 0)),
                  pl.BlockSpec((2, d), lambda i: (0, 0)),
                  pl.BlockSpec((2, d), lambda i: (0, 0)),
                  pl.BlockSpec(memory_space=pl.ANY)],
        out_specs=pl.BlockSpec(memory_space=pl.ANY),
        scratch_shapes=[pltpu.VMEM((tm, d), F32), pltpu.VMEM((tm, d // 2), U32),
                        pltpu.SemaphoreType.DMA((2,))] + _norm_scratch(d),
        input_output_aliases={6: 0},
        compiler_params=_params(("arbitrary",)),
        name="moe_dispatch",
    )(d1.reshape(nt, 1, tm), d2.reshape(nt, 1, tm), x, g2, sh, sc, jnp.zeros((n_rows, d // 2), U32))


def _combine_kernel(d1_ref, d2_ref, mf_ref, x_ref, gt_ref, yb_ref, o_ref, buf, sem, *, n_lat):
    i = pl.program_id(0)
    tm = x_ref.shape[0]

    def start(r, c):
        _row_copy(yb_ref, d1_ref[0, r], buf.at[0], r, sem.at[0]).start()
        _row_copy(yb_ref, d2_ref[0, r], buf.at[1], r, sem.at[1]).start()
        return c

    def wait(r, c):
        _row_copy(yb_ref, d1_ref[0, r], buf.at[0], r, sem.at[0]).wait()
        _row_copy(yb_ref, d2_ref[0, r], buf.at[1], r, sem.at[1]).wait()
        return c

    lax.fori_loop(0, tm, start, 0, unroll=ROW_DMA_UNROLL)
    lax.fori_loop(0, tm, wait, 0, unroll=ROW_DMA_UNROLL)
    f = mf_ref[:, 0:1] * buf[0] + mf_ref[:, 1:2] * buf[1]
    o_ref[...] = x_ref[...] + _row_select(gt_ref, i * tm, tm, n_lat) * f


def _combine(x, gt, yb, d1, d2, mf, n_lat, n_out):
    r, d = x.shape
    tm = ROW_TILE
    nt = r // tm
    smem_rows = pl.BlockSpec((None, 1, tm), lambda i: (i, 0, 0), memory_space=pltpu.SMEM)
    return pl.pallas_call(
        functools.partial(_combine_kernel, n_lat=n_lat),
        out_shape=jax.ShapeDtypeStruct((n_out, d), F32),
        grid=(n_out // tm,),
        in_specs=[smem_rows, smem_rows,
                  pl.BlockSpec((tm, V7X_LANES), lambda i: (i, 0)),
                  pl.BlockSpec((tm, d), lambda i: (i, 0)),
                  pl.BlockSpec((2, d), lambda i: (0, 0)),
                  pl.BlockSpec(memory_space=pl.ANY)],
        out_specs=pl.BlockSpec((tm, d), lambda i: (i, 0)),
        scratch_shapes=[pltpu.VMEM((2, tm, d), F32), pltpu.SemaphoreType.DMA((2,))],
        compiler_params=_params(("arbitrary",)),
        name="moe_combine",
    )(d1.reshape(nt, 1, tm), d2.reshape(nt, 1, tm), mf, x, gt, yb)


def _moe_ffn(x, g2, sh, sc, gt, wr, br, w1, w3, w2, layer, n_lat, n_out):
    r, d = x.shape
    n_exp = wr.shape[1]
    blk = MOE_BLOCK
    wr_pad = jnp.zeros((d, V7X_LANES), F32).at[:, :n_exp].set(wr)
    br_pad = jnp.zeros((1, V7X_LANES), F32).at[0, :n_exp].set(br)
    mi, mf, cnt = _route(x, g2, sh, sc, wr_pad, br_pad, n_lat, n_exp)

    counts = cnt[0, :n_exp].astype(I32)
    padded = (counts + blk - 1) // blk * blk
    pad_end = jnp.cumsum(padded)
    pad_start = pad_end - padded
    d1 = pad_start[mi[:, 0]] + mi[:, 2]
    d2 = pad_start[mi[:, 1]] + mi[:, 3]
    n_rows = -(-(TOP_K * r + n_exp * (blk - 1)) // blk) * blk
    blk_lo = (pad_start // blk).astype(I32)
    blk_hi = (pad_end // blk).astype(I32)

    xb = _dispatch(x, g2, sh, sc, d1, d2, n_rows, n_lat)
    gact = _swiglu_up(xb, blk_lo, blk_hi, w1, w3, layer, blk)
    yb = _swiglu_down(gact, blk_lo, blk_hi, w2, layer, blk)
    return _combine(x, gt, yb, d1, d2, mf, n_lat, n_out)


def _rope_tables(n_lat, n_ctx):
    n_rows = n_lat // GRID_W
    axis_dim = HEAD_DIM // 2
    inv_freq = ROPE_THETA ** (-jnp.arange(0, axis_dim, 2, dtype=F32) / axis_dim)

    def axis_tables(n):
        ang = jnp.arange(n, dtype=F32)[:, None] * inv_freq[None, :]
        return jnp.cos(ang), jnp.sin(ang)

    per_row = lambda t: jnp.repeat(t, GRID_W, axis=0)
    per_col = lambda t: jnp.tile(t, (n_rows, 1))
    cr, sr = map(per_row, axis_tables(n_rows))
    cc, sc = map(per_col, axis_tables(GRID_W))
    cos_t = jnp.concatenate([cr, cr, cc, cc], axis=1)
    sin_t = jnp.concatenate([-sr, sr, -sc, sc], axis=1)
    cos_t = jnp.concatenate([cos_t, jnp.ones((n_ctx, HEAD_DIM), F32)], axis=0)
    sin_t = jnp.concatenate([sin_t, jnp.zeros((n_ctx, HEAD_DIM), F32)], axis=0)
    return cos_t, sin_t


def kernel(x, c, ctx, c_ctx, w_ada, b_ada, g_norm1, w_in, conv_w, conv_b, conv_ln_g, conv_ln_b, conv_pw, conv_pw_b, sgu_ln_g, sgu_ln_b, sgu_w, sgu_b, swa_q_g, swa_k_g, swa_sink, glb_q_g, glb_k_g, g_branch, w_out, g_norm2, ffn_w1, ffn_w3, ffn_w2, router_w, router_b, exp_w1, exp_w3, exp_w2):
    batch, n_lat, d = x.shape
    n_ctx = ctx.shape[1]
    depth = w_ada.shape[0]
    gw = d // N_GROUPS
    assert batch == 1 and n_lat % n_ctx == 0 and n_ctx % ROW_TILE == 0 and n_lat % GRID_W == 0
    assert conv_w.shape[1] // 2 < CONV_HALO and w_in.shape[2] == 4 * 2 * gw

    xs = jnp.concatenate([x[0], ctx[0]], axis=0)
    mods = _ada(jnp.stack([c[0], c_ctx], axis=1), w_ada, b_ada)
    cos_t, sin_t = _rope_tables(n_lat, n_ctx)
    row2 = lambda v: v.reshape(1, -1)

    for l in range(depth):
        sh1, sc1, gt1, sh2, sc2, gt2 = (mods[l, :, k * d:(k + 1) * d] for k in range(6))

        p = _inproj(xs, row2(g_norm1[l]), sh1, sc1, w_in[l].astype(BF16), n_lat)
        y_conv = _conv_group(p, conv_w[l], row2(conv_b[l]), row2(conv_ln_g[l]), row2(conv_ln_b[l]),
                             conv_pw[l].astype(BF16), row2(conv_pw_b[l]), n_lat)
        y_sgu = _sgu_group(p, row2(sgu_ln_g[l]), row2(sgu_ln_b[l]), sgu_w[l].astype(BF16), sgu_b[l].T)
        qt_s, k_s, vt_s, _ = _prep(p, row2(swa_q_g[l]), row2(swa_k_g[l]), cos_t, sin_t, block=2, feature_major=True)
        y_swa = _swa(qt_s, k_s, vt_s, swa_sink[l], n_lat)
        qt_g, k_g, vt_g, kn = _prep(p, row2(glb_q_g[l]), row2(glb_k_g[l]), cos_t, sin_t, block=3, feature_major=True)
        kmax = jnp.sqrt(jnp.max(kn[:, 0, :N_KV_HEADS], axis=0)) * KEY_NORM_MARGIN
        y_glb = _glb(qt_g, k_g, vt_g, kmax, n_lat)
        xs = _outproj((y_conv, y_sgu, y_swa, y_glb), row2(g_branch[l]), w_out[l].astype(BF16), xs, gt1, n_lat)

        if l % 2 == 0:
            xs = _dense_ffn(xs, row2(g_norm2[l]), sh2, sc2, gt2, ffn_w1, ffn_w3, ffn_w2, l // 2, n_lat)
        else:
            xs = _moe_ffn(xs, row2(g_norm2[l]), sh2, sc2, gt2, router_w[l // 2], router_b[l // 2],
                          exp_w1, exp_w3, exp_w2, l // 2, n_lat, n_out=n_lat if l == depth - 1 else n_lat + n_ctx)
    return xs[:n_lat][None]
```

```python
import functools
import math

import jax
import jax.numpy as jnp
from jax import lax
from jax.experimental import pallas as pl
from jax.experimental.pallas import tpu as pltpu

F32 = jnp.float32
BF16 = jnp.bfloat16
I32 = jnp.int32

HEAD_DIM = 128
N_GROUPS = 4
N_KV_HEADS = 2
GRID_W = 64
WINDOW = 128
ROPE_THETA = 10000.0
TOP_K = 2
MOE_BLOCK = 512
EPS = 1e-6
NEG = -1e30
SCALE = HEAD_DIM ** -0.5
Q_PRESCALE = SCALE * math.log2(math.e)
EXP2_SAFE_SHIFT = 60.0
KEY_NORM_MARGIN = 1.01

V7X_VMEM_BYTES = 64 * 1024 * 1024
V7X_LANES = 128
V7X_SUBLANES = 8
VMEM_BUDGET = V7X_VMEM_BYTES - 8 * 1024 * 1024

ROW_TILE = 256
CONV_HALO = 16
NORM_CHUNK = 16
NORM_UNROLL = 8
ROW_DMA_UNROLL = 8


def _pick(n, cands):
    for c in cands:
        if n % c == 0:
            return c
    raise ValueError(f"no tile in {cands} divides {n}")


def _params(sem, vmem=VMEM_BUDGET):
    return pltpu.CompilerParams(dimension_semantics=sem, vmem_limit_bytes=vmem)


def _sigmoid(x):
    return 1.0 / (1.0 + jnp.exp(-x))


def _silu(x):
    return x * _sigmoid(x)


def _gelu_tanh(x):
    c = math.sqrt(2.0 / math.pi)
    return 0.5 * x * (1.0 + jnp.tanh(c * (x + 0.044715 * (x * x * x))))


def _rms(x):
    return x * lax.rsqrt(jnp.mean(x * x, axis=-1, keepdims=True) + EPS)


def _layer_norm(x, g, b):
    xc = x - jnp.mean(x, axis=-1, keepdims=True)
    return xc * lax.rsqrt(jnp.mean(xc * xc, axis=-1, keepdims=True) + EPS) * g + b


def _row_select(vec2_ref, row0, tm, n_lat):
    rows = row0 + lax.broadcasted_iota(I32, (tm, 1), 0)
    return jnp.where(rows >= n_lat, vec2_ref[1:2, :], vec2_ref[0:1, :])


def _norm_mod(x, g, shift, scale):
    return _rms(x) * g * (1.0 + scale) + shift


def _norm_scratch(d):
    return [pltpu.VMEM((2 * V7X_SUBLANES, d), F32), pltpu.VMEM((2 * V7X_SUBLANES, d), F32)]


def _norm_mod_rows(x_ref, g_ref, sh_ref, sc_ref, out_ref, gs_s, sh_s, row0, n_lat):
    tm, d = x_ref.shape
    sub = V7X_SUBLANES
    for t in range(2):
        gs_s[t * sub:(t + 1) * sub, :] = jnp.broadcast_to(g_ref[...] * (1.0 + sc_ref[t:t + 1, :]), (sub, d))
        sh_s[t * sub:(t + 1) * sub, :] = jnp.broadcast_to(sh_ref[t:t + 1, :], (sub, d))

    def body(c, carry):
        r0 = pl.multiple_of(c * NORM_CHUNK, NORM_CHUNK)
        t0 = pl.multiple_of(jnp.where(row0 + r0 >= n_lat, sub, 0), sub)
        x = x_ref[pl.ds(r0, NORM_CHUNK), :].reshape(NORM_CHUNK // sub, sub, d)
        y = _rms(x) * gs_s[pl.ds(t0, sub), :] + sh_s[pl.ds(t0, sub), :]
        out_ref[pl.ds(r0, NORM_CHUNK), :] = y.reshape(NORM_CHUNK, d).astype(out_ref.dtype)
        return carry

    lax.fori_loop(0, tm // NORM_CHUNK, body, 0, unroll=NORM_UNROLL)


def _ada_kernel(s_ref, w_ref, b_ref, o_ref, act_s):
    d, tn = w_ref.shape
    kc = 64
    act_s[...] = _silu(s_ref[...])

    def body(c, acc):
        a0, a1 = acc
        k0 = pl.multiple_of(c * kc, kc)
        w = w_ref[pl.ds(k0, kc), :]
        s = act_s[pl.ds(k0, kc), :]
        a0 = a0 + (w * s[:, 0:1]).reshape(kc // V7X_SUBLANES, V7X_SUBLANES, tn).sum(axis=0)
        a1 = a1 + (w * s[:, 1:2]).reshape(kc // V7X_SUBLANES, V7X_SUBLANES, tn).sum(axis=0)
        return a0, a1

    z = jnp.zeros((V7X_SUBLANES, tn), F32)
    a0, a1 = lax.fori_loop(0, d // kc, body, (z, z), unroll=4)
    o_ref[0:1, :] = a0.sum(axis=0, keepdims=True) + b_ref[...]
    o_ref[1:2, :] = a1.sum(axis=0, keepdims=True) + b_ref[...]


def _ada(cond, w_ada, b_ada):
    depth, d, n = w_ada.shape
    tn = _pick(n, (1024, 512, 256, 128))
    return pl.pallas_call(
        _ada_kernel,
        out_shape=jax.ShapeDtypeStruct((depth, 2, n), F32),
        grid=(depth, n // tn),
        in_specs=[
            pl.BlockSpec((d, 2), lambda l, j: (0, 0)),
            pl.BlockSpec((None, d, tn), lambda l, j: (l, 0, j)),
            pl.BlockSpec((None, 1, tn), lambda l, j: (l, 0, j)),
        ],
        out_specs=pl.BlockSpec((None, 2, tn), lambda l, j: (l, 0, j)),
        scratch_shapes=[pltpu.VMEM((d, 2), F32)],
        compiler_params=_params(("parallel", "parallel")),
        name="ada",
    )(cond, w_ada, b_ada.reshape(depth, 1, n))


def _weight_spec(k, n, itemsize, other_bytes):
    if other_bytes + k * n * itemsize <= VMEM_BUDGET:
        return n, pl.BlockSpec((k, n), lambda i, j: (0, 0), pipeline_mode=pl.Buffered(1))
    tn = _pick(n, (1024, 512, 256))
    return tn, pl.BlockSpec((k, tn), lambda i, j: (0, j))


def _inproj_kernel(x_ref, g_ref, sh_ref, sc_ref, w_ref, o_ref, h_s, gs_s, sh_s, *, n_lat):
    i = pl.program_id(0)
    tm = x_ref.shape[0]

    @pl.when(pl.program_id(1) == 0)
    def _():
        _norm_mod_rows(x_ref, g_ref, sh_ref, sc_ref, h_s, gs_s, sh_s, i * tm, n_lat)

    o_ref[...] = jnp.dot(h_s[...], w_ref[...], preferred_element_type=F32)


def _inproj(x, g, sh, sc, w, n_lat):
    r, d = x.shape
    n = w.shape[1]
    tm = _pick(r, (640, 256))
    tn, w_spec = _weight_spec(d, n, w.dtype.itemsize, 2 * tm * d * 4 + 2 * tm * n * 4 + tm * d * 2)
    return pl.pallas_call(
        functools.partial(_inproj_kernel, n_lat=n_lat),
        out_shape=jax.ShapeDtypeStruct((r, n), F32),
        grid=(r // tm, n // tn),
        in_specs=[
            pl.BlockSpec((tm, d), lambda i, j: (i, 0)),
            pl.BlockSpec((1, d), lambda i, j: (0, 0)),
            pl.BlockSpec((2, d), lambda i, j: (0, 0)),
            pl.BlockSpec((2, d), lambda i, j: (0, 0)),
            w_spec,
        ],
        out_specs=pl.BlockSpec((tm, tn), lambda i, j: (i, j)),
        scratch_shapes=[pltpu.VMEM((tm, d), BF16)] + _norm_scratch(d),
        compiler_params=_params(("parallel", "arbitrary")),
        name="inproj",
    )(x, g, sh, sc, w)


def _conv_kernel(pm_ref, pp_ref, pn_ref, cw_ref, cb_ref, lg_ref, lb_ref, pw_ref, pb_ref, o_ref, ext_s, sft_s,
                 *, n_lat_tiles, n_tiles):
    i = pl.program_id(0)
    tm, gw = o_ref.shape
    kw = cw_ref.shape[0]

    def glu(p):
        return p[:, :gw] * _sigmoid(p[:, gw:])

    prev_ok = jnp.logical_and(i != 0, i != n_lat_tiles)
    next_ok = jnp.logical_and(i != n_lat_tiles - 1, i != n_tiles - 1)
    ext_s[0:CONV_HALO, :] = jnp.where(prev_ok, glu(pp_ref[...]), 0.0)
    ext_s[CONV_HALO:CONV_HALO + tm, :] = glu(pm_ref[...])
    ext_s[CONV_HALO + tm:, :] = jnp.where(next_ok, glu(pn_ref[...]), 0.0)

    sub = V7X_SUBLANES
    span = sft_s.shape[1]
    for b in range(1, sub):
        sft_s[b - 1] = ext_s[b:b + span, :]

    acc = jnp.zeros((tm, gw), F32) + cb_ref[...]
    for k in range(kw):
        a, b = divmod(CONV_HALO - kw // 2 + k, sub)
        tap = ext_s[a * sub:a * sub + tm, :] if b == 0 else sft_s[b - 1, a * sub:a * sub + tm, :]
        acc = acc + cw_ref[k:k + 1, :] * tap
    y = _silu(_layer_norm(acc, lg_ref[...], lb_ref[...]))
    o_ref[...] = jnp.dot(y.astype(BF16), pw_ref[...], preferred_element_type=F32) + pb_ref[...]


def _conv_group(p, cw, cb, lg, lb, pw, pb, n_lat):
    r = p.shape[0]
    kw, gw = cw.shape
    tm = ROW_TILE
    n_tiles = r // tm
    hb = tm // CONV_HALO
    last_hb = r // CONV_HALO - 1
    return pl.pallas_call(
        functools.partial(_conv_kernel, n_lat_tiles=n_lat // tm, n_tiles=n_tiles),
        out_shape=jax.ShapeDtypeStruct((r, gw), F32),
        grid=(n_tiles,),
        in_specs=[
            pl.BlockSpec((tm, 2 * gw), lambda i: (i, 0)),
            pl.BlockSpec((CONV_HALO, 2 * gw), lambda i: (jnp.maximum(i * hb - 1, 0), 0)),
            pl.BlockSpec((CONV_HALO, 2 * gw), lambda i: (jnp.minimum((i + 1) * hb, last_hb), 0)),
            pl.BlockSpec((kw, gw), lambda i: (0, 0)),
            pl.BlockSpec((1, gw), lambda i: (0, 0)),
            pl.BlockSpec((1, gw), lambda i: (0, 0)),
            pl.BlockSpec((1, gw), lambda i: (0, 0)),
            pl.BlockSpec((gw, gw), lambda i: (0, 0)),
            pl.BlockSpec((1, gw), lambda i: (0, 0)),
        ],
        out_specs=pl.BlockSpec((tm, gw), lambda i: (i, 0)),
        scratch_shapes=[pltpu.VMEM((tm + 2 * CONV_HALO, gw), F32),
                        pltpu.VMEM((V7X_SUBLANES - 1, tm + 2 * CONV_HALO - V7X_SUBLANES, gw), F32)],
        compiler_params=_params(("parallel",)),
        name="conv_group",
    )(p, p, p, cw, cb, lg, lb, pw, pb)


def _sgu_kernel(p_ref, lg_ref, lb_ref, ws_ref, bs_ref, o_ref):
    tm, gw = o_ref.shape
    n_h, ch, _ = ws_ref.shape
    hd = gw // n_h
    z = _gelu_tanh(p_ref[...])
    u = z[:, :gw]
    v = _layer_norm(z[:, gw:], lg_ref[...], lb_ref[...]).astype(BF16)
    for c in range(tm // ch):
        rows = slice(c * ch, (c + 1) * ch)
        parts = []
        for h in range(n_h):
            s = jnp.dot(ws_ref[h], v[rows, h * hd:(h + 1) * hd], preferred_element_type=F32)
            parts.append(s + bs_ref[:, h:h + 1])
        o_ref[rows, :] = u[rows, :] * jnp.concatenate(parts, axis=1)


def _sgu_group(p, lg, lb, ws, bs_t):
    r = p.shape[0]
    gw = lg.shape[1]
    n_h, ch, _ = ws.shape
    tm = ROW_TILE
    return pl.pallas_call(
        _sgu_kernel,
        out_shape=jax.ShapeDtypeStruct((r, gw), F32),
        grid=(r // tm,),
        in_specs=[
            pl.BlockSpec((tm, 2 * gw), lambda i: (i, 1)),
            pl.BlockSpec((1, gw), lambda i: (0, 0)),
            pl.BlockSpec((1, gw), lambda i: (0, 0)),
            pl.BlockSpec((n_h, ch, ch), lambda i: (0, 0, 0)),
            pl.BlockSpec((ch, n_h), lambda i: (0, 0)),
        ],
        out_specs=pl.BlockSpec((tm, gw), lambda i: (i, 0)),
        compiler_params=_params(("parallel",)),
        name="sgu_group",
    )(p, lg, lb, ws, bs_t)


def _prep_kernel(p_ref, qg_ref, kg_ref, cos_ref, sin_ref, qt_ref, k_ref, vt_ref, kn_ref):
    tm = p_ref.shape[0]
    kvw = k_ref.shape[1]
    qw = p_ref.shape[1] - 2 * kvw
    cos = cos_ref[...]
    sin = sin_ref[...]
    lane = lax.broadcasted_iota(I32, (tm, HEAD_DIM), 1)
    low = (lane & (HEAD_DIM // 4)) == 0

    def head(x, g, mult):
        y = _rms(x) * g
        partner = jnp.where(low, pltpu.roll(y, HEAD_DIM - HEAD_DIM // 4, 1), pltpu.roll(y, HEAD_DIM // 4, 1))
        return (y * cos + partner * sin) * mult

    eye = jnp.where(lax.broadcasted_iota(I32, (HEAD_DIM, HEAD_DIM), 0)
                    == lax.broadcasted_iota(I32, (HEAD_DIM, HEAD_DIM), 1), 1.0, 0.0).astype(BF16)

    def put_t(ref, h, val):
        ref[h * HEAD_DIM:(h + 1) * HEAD_DIM, :] = _nt_dot(eye, val.astype(BF16)).astype(BF16)

    for h in range(qw // HEAD_DIM):
        put_t(qt_ref, h, head(p_ref[:, h * HEAD_DIM:(h + 1) * HEAD_DIM], qg_ref[...], Q_PRESCALE))
    kn_lane = lax.broadcasted_iota(I32, kn_ref.shape, 1)
    kn = jnp.zeros(kn_ref.shape, F32)
    for h in range(kvw // HEAD_DIM):
        kh = head(p_ref[:, qw + h * HEAD_DIM:qw + (h + 1) * HEAD_DIM], kg_ref[...], 1.0)
        k_ref[:, h * HEAD_DIM:(h + 1) * HEAD_DIM] = kh.astype(BF16)
        put_t(vt_ref, h, p_ref[:, qw + kvw + h * HEAD_DIM:qw + kvw + (h + 1) * HEAD_DIM])
        ksq = jnp.sum(kh * kh, axis=1, keepdims=True).max(axis=0, keepdims=True)
        kn = jnp.where(kn_lane == h, ksq, kn)
    kn_ref[...] = kn


def _prep(p, qg, kg, cos_t, sin_t, block):
    r = p.shape[0]
    qw = N_GROUPS * HEAD_DIM
    kvw = N_KV_HEADS * HEAD_DIM
    tm = ROW_TILE
    out_shape = (jax.ShapeDtypeStruct((qw, r), BF16), jax.ShapeDtypeStruct((r, kvw), BF16),
                 jax.ShapeDtypeStruct((kvw, r), BF16),
                 jax.ShapeDtypeStruct((r // tm, V7X_SUBLANES, V7X_LANES), F32))
    out_specs = (pl.BlockSpec((qw, tm), lambda i: (0, i)), pl.BlockSpec((tm, kvw), lambda i: (i, 0)),
                 pl.BlockSpec((kvw, tm), lambda i: (0, i)),
                 pl.BlockSpec((None, V7X_SUBLANES, V7X_LANES), lambda i: (i, 0, 0)))
    return pl.pallas_call(
        _prep_kernel,
        out_shape=out_shape,
        grid=(r // tm,),
        in_specs=[
            pl.BlockSpec((tm, qw + 2 * kvw), lambda i: (i, block)),
            pl.BlockSpec((1, HEAD_DIM), lambda i: (0, 0)),
            pl.BlockSpec((1, HEAD_DIM), lambda i: (0, 0)),
            pl.BlockSpec((tm, HEAD_DIM), lambda i: (i, 0)),
            pl.BlockSpec((tm, HEAD_DIM), lambda i: (i, 0)),
        ],
        out_specs=out_specs,
        compiler_params=_params(("parallel",)),
        name="qkv_prep",
    )(p, qg, kg, cos_t, sin_t)


def _nt_dot(a, b):
    return lax.dot_general(a, b, (((1,), (1,)), ((), ())), preferred_element_type=F32)


def _swa_kernel(sink_ref, qt_ref, km_ref, kp_ref, kn_ref, vm_ref, vp_ref, vn_ref, kc_ref, vc_ref, o_ref,
                *, n_lat):
    i = pl.program_id(0)
    tq = qt_ref.shape[1]
    hb = kp_ref.shape[0]
    n_rep = (qt_ref.shape[0] // HEAD_DIM) // N_KV_HEADS
    nk = tq + 2 * hb
    kj = lax.broadcasted_iota(I32, (nk, tq), 0) - hb
    qi = lax.broadcasted_iota(I32, (nk, tq), 1)
    kglob = i * tq + kj
    ok1 = (jnp.abs(kj - qi) <= WINDOW) & (kglob >= 0) & (kglob < n_lat) & (i * tq < n_lat)
    ok = jnp.concatenate([ok1.astype(F32)] * n_rep, axis=1) > 0.5
    q_lane = lax.broadcasted_iota(I32, (1, n_rep * tq), 1)
    for j in range(N_KV_HEADS):
        kv = slice(j * HEAD_DIM, (j + 1) * HEAD_DIM)
        q2t = jnp.concatenate(
            [qt_ref[(j * n_rep + g) * HEAD_DIM:(j * n_rep + g + 1) * HEAD_DIM, :] for g in range(n_rep)], axis=1)
        kw = jnp.concatenate([kp_ref[:, kv], km_ref[:, kv], kn_ref[:, kv]], axis=0)
        vwt = jnp.concatenate([vp_ref[kv, :], vm_ref[kv, :], vn_ref[kv, :]], axis=1)
        s_w = jnp.where(ok, jnp.dot(kw, q2t, preferred_element_type=F32), NEG)
        s_c = jnp.dot(kc_ref[:, kv], q2t, preferred_element_type=F32)
        sk = jnp.zeros((1, n_rep * tq), F32)
        for g in range(n_rep):
            sk = jnp.where(q_lane >= g * tq, sink_ref[j * n_rep + g] * math.log2(math.e), sk)
        m = jnp.maximum(jnp.maximum(s_w.max(axis=0, keepdims=True), s_c.max(axis=0, keepdims=True)), sk)
        p_w = jnp.exp2(s_w - m)
        p_c = jnp.exp2(s_c - m)
        den = jnp.exp2(sk - m) + p_w.sum(axis=0, keepdims=True) + p_c.sum(axis=0, keepdims=True)
        ot = (jnp.dot(vwt, p_w.astype(BF16), preferred_element_type=F32)
              + jnp.dot(vc_ref[kv, :], p_c.astype(BF16), preferred_element_type=F32)) / den
        o = ot.T
        for g in range(n_rep):
            h = j * n_rep + g
            o_ref[:, h * HEAD_DIM:(h + 1) * HEAD_DIM] = o[g * tq:(g + 1) * tq, :]


def _swa(qt, k, vt, sink, n_lat):
    qw = qt.shape[0]
    r, kvw = k.shape
    n_ctx = r - n_lat
    tq = ROW_TILE
    hb = WINDOW
    per = tq // hb
    last_hb = r // hb - 1
    prev_b = lambda i: jnp.maximum(i * per - 1, 0)
    next_b = lambda i: jnp.minimum((i + 1) * per, last_hb)
    return pl.pallas_call(
        functools.partial(_swa_kernel, n_lat=n_lat),
        out_shape=jax.ShapeDtypeStruct((r, qw), F32),
        grid=(r // tq,),
        in_specs=[
            pl.BlockSpec(memory_space=pltpu.SMEM),
            pl.BlockSpec((qw, tq), lambda i: (0, i)),
            pl.BlockSpec((tq, kvw), lambda i: (i, 0)),
            pl.BlockSpec((hb, kvw), lambda i: (prev_b(i), 0)),
            pl.BlockSpec((hb, kvw), lambda i: (next_b(i), 0)),
            pl.BlockSpec((kvw, tq), lambda i: (0, i)),
            pl.BlockSpec((kvw, hb), lambda i: (0, prev_b(i))),
            pl.BlockSpec((kvw, hb), lambda i: (0, next_b(i))),
            pl.BlockSpec((n_ctx, kvw), lambda i: (n_lat // n_ctx, 0)),
            pl.BlockSpec((kvw, n_ctx), lambda i: (0, n_lat // n_ctx)),
        ],
        out_specs=pl.BlockSpec((tq, qw), lambda i: (i, 0)),
        compiler_params=_params(("parallel",)),
        name="window_attn",
    )(sink, qt, k, k, k, vt, vt, vt, k, vt)


def _glb_kernel(kmax_ref, qt_ref, k_ref, vt_ref, o_ref, acc_s, l_s, *, n_lat, tk, unroll):
    j = pl.program_id(0)
    i = pl.program_id(1)
    tq = qt_ref.shape[1]
    n_rep = qt_ref.shape[0] // HEAD_DIM
    nq = n_rep * tq
    n_ctx = k_ref.shape[0] - n_lat
    q2t = jnp.concatenate([qt_ref[g * HEAD_DIM:(g + 1) * HEAD_DIM, :] for g in range(n_rep)], axis=1)
    qf = q2t.astype(F32)
    bound = jnp.sqrt(jnp.sum(qf * qf, axis=0, keepdims=True)) * kmax_ref[j]

    def scores(k0, size):
        return jnp.dot(k_ref[pl.ds(k0, size), :], q2t, preferred_element_type=F32)

    def shifted_by_bound():
        def add(k0, size):
            p = jnp.exp2(scores(k0, size) - bound)
            l_s[...] += p.sum(axis=0, keepdims=True)
            acc_s[...] += jnp.dot(vt_ref[:, pl.ds(k0, size)], p.astype(BF16), preferred_element_type=F32)

        acc_s[...] = jnp.zeros_like(acc_s)
        l_s[...] = jnp.zeros_like(l_s)
        add(n_lat, n_ctx)

        @pl.when(i * tq < n_lat)
        def _():
            def body(c, carry):
                for u in range(unroll):
                    add(pl.multiple_of((c * unroll + u) * tk, tk), tk)
                return carry

            lax.fori_loop(0, n_lat // tk // unroll, body, 0)

    def shifted_by_running_max():
        def update(s, vt, m, l):
            m_new = jnp.maximum(m, s.max(axis=0, keepdims=True))
            alpha = jnp.exp2(m - m_new)
            p = jnp.exp2(s - m_new)
            l = alpha * l + p.sum(axis=0, keepdims=True)
            acc_s[...] = alpha * acc_s[...] + jnp.dot(vt, p.astype(BF16), preferred_element_type=F32)
            return m_new, l

        acc_s[...] = jnp.zeros_like(acc_s)
        first = update(scores(n_lat, n_ctx), vt_ref[:, n_lat:],
                       jnp.full((1, nq), NEG, F32), jnp.zeros((1, nq), F32))

        def body(c, carry):
            k0 = pl.multiple_of(c * tk, tk)
            return update(scores(k0, tk), vt_ref[:, pl.ds(k0, tk)], *carry)

        _, l = lax.fori_loop(0, jnp.where(i * tq < n_lat, n_lat // tk, 0), body, first)
        l_s[...] = l

    lax.cond(jnp.max(bound) <= EXP2_SAFE_SHIFT, shifted_by_bound, shifted_by_running_max)
    o = (acc_s[...] / l_s[...]).T
    for g in range(n_rep):
        o_ref[:, g * HEAD_DIM:(g + 1) * HEAD_DIM] = o[g * tq:(g + 1) * tq, :]


def _glb(qt, k, vt, kmax, n_lat):
    r, kvw = k.shape
    qw = qt.shape[0]
    tq = ROW_TILE
    gq = qw // N_KV_HEADS
    tk = _pick(n_lat, (2048, 1024, 512, 256))
    unroll = 2 if (n_lat // tk) % 2 == 0 else 1
    nq = gq // HEAD_DIM * tq
    return pl.pallas_call(
        functools.partial(_glb_kernel, n_lat=n_lat, tk=tk, unroll=unroll),
        out_shape=jax.ShapeDtypeStruct((r, qw), F32),
        grid=(N_KV_HEADS, r // tq),
        in_specs=[
            pl.BlockSpec(memory_space=pltpu.SMEM),
            pl.BlockSpec((gq, tq), lambda j, i: (j, i)),
            pl.BlockSpec((r, HEAD_DIM), lambda j, i: (0, j)),
            pl.BlockSpec((HEAD_DIM, r), lambda j, i: (j, 0)),
        ],
        out_specs=pl.BlockSpec((tq, gq), lambda j, i: (i, j)),
        scratch_shapes=[pltpu.VMEM((HEAD_DIM, nq), F32), pltpu.VMEM((1, nq), F32)],
        compiler_params=_params(("parallel", "parallel")),
        name="global_attn",
    )(kmax, qt, k, vt)


def _outproj_kernel(y0_ref, y1_ref, y2_ref, y3_ref, gb_ref, w_ref, x_ref, gt_ref, o_ref, yn_s, *, n_lat):
    i = pl.program_id(0)
    tm = x_ref.shape[0]

    @pl.when(pl.program_id(1) == 0)
    def _():
        for g, y_ref in enumerate((y0_ref, y1_ref, y2_ref, y3_ref)):
            gw = y_ref.shape[1]
            cols = slice(g * gw, (g + 1) * gw)
            yn_s[:, cols] = (_rms(y_ref[...]) * gb_ref[:, cols]).astype(BF16)

    gate = _row_select(gt_ref, i * tm, tm, n_lat)
    o_ref[...] = x_ref[...] + gate * jnp.dot(yn_s[...], w_ref[...], preferred_element_type=F32)


def _outproj(ys, gb, w, x, gt, n_lat):
    r, d = x.shape
    gw = ys[0].shape[1]
    tm = _pick(r, (640, 256))
    tn, w_spec = _weight_spec(d, d, w.dtype.itemsize, 2 * tm * d * 4 * 3 + tm * d * 2)
    ysp = pl.BlockSpec((tm, gw), lambda i, j: (i, 0))
    return pl.pallas_call(
        functools.partial(_outproj_kernel, n_lat=n_lat),
        out_shape=jax.ShapeDtypeStruct((r, d), F32),
        grid=(r // tm, d // tn),
        in_specs=[ysp, ysp, ysp, ysp,
                  pl.BlockSpec((1, d), lambda i, j: (0, 0)),
                  w_spec,
                  pl.BlockSpec((tm, tn), lambda i, j: (i, j)),
                  pl.BlockSpec((2, tn), lambda i, j: (0, j))],
        out_specs=pl.BlockSpec((tm, tn), lambda i, j: (i, j)),
        scratch_shapes=[pltpu.VMEM((tm, d), BF16)],
        compiler_params=_params(("parallel", "arbitrary")),
        name="outproj",
    )(*ys, gb, w, x, gt)


def _ffn_norm_kernel(x_ref, g_ref, sh_ref, sc_ref, o_ref, gs_s, sh_s, *, n_lat):
    _norm_mod_rows(x_ref, g_ref, sh_ref, sc_ref, o_ref, gs_s, sh_s, pl.program_id(0) * x_ref.shape[0], n_lat)


def _ffn_norm(x, g, sh, sc, n_lat):
    r, d = x.shape
    tm = _pick(r, (640, 256))
    return pl.pallas_call(
        functools.partial(_ffn_norm_kernel, n_lat=n_lat),
        out_shape=jax.ShapeDtypeStruct((r, d), BF16),
        grid=(r // tm,),
        in_specs=[pl.BlockSpec((tm, d), lambda i: (i, 0)),
                  pl.BlockSpec((1, d), lambda i: (0, 0)),
                  pl.BlockSpec((2, d), lambda i: (0, 0)),
                  pl.BlockSpec((2, d), lambda i: (0, 0))],
        out_specs=pl.BlockSpec((tm, d), lambda i: (i, 0)),
        scratch_shapes=_norm_scratch(d),
        compiler_params=_params(("parallel",)),
        name="ffn_norm",
    )(x, g, sh, sc)


U32 = jnp.uint32
BF16_HI_MASK = 0xFFFF0000


def _pack_bf16_pairs(lo, hi):
    lo_bits = pltpu.bitcast(lo.astype(BF16).astype(F32), U32)
    hi_bits = pltpu.bitcast(hi.astype(BF16).astype(F32), U32)
    return hi_bits | (lo_bits >> 16)


def _unpack_bf16_pairs(packed):
    lo = pltpu.bitcast(packed << 16, F32).astype(BF16)
    hi = pltpu.bitcast(packed & U32(BF16_HI_MASK), F32).astype(BF16)
    return lo, hi


def _fit_tile(n, cands, vmem_bytes):
    for c in cands:
        if n % c == 0 and vmem_bytes(c) <= VMEM_BUDGET:
            return c
    raise ValueError(f"no tile in {cands} divides {n} within the VMEM budget")


def _sweep_blocks(lo, hi, n_live, p, n_pass, loads, compute, stores):
    def start(copies, priority=0):
        for cp in copies:
            cp.start(priority=priority)

    def wait(copies):
        for cp in copies:
            cp.wait()

    t_end = n_pass * n_live
    load_priority = 1

    @pl.when(jnp.logical_and(p == 0, jnp.logical_and(lo == 0, hi > 0)))
    def _():
        start(loads(0, 0, 0), load_priority)

    def body(b, carry):
        t = p * n_live + b
        slot = t & 1
        wait(loads(b, p, slot))
        wrap = b + 1 >= n_live

        @pl.when(t + 1 < t_end)
        def _():
            start(loads(jnp.where(wrap, 0, b + 1), jnp.where(wrap, p + 1, p), 1 - slot), load_priority)

        @pl.when(t >= 2)
        def _():
            wait(stores(b, p, slot))
        compute(b, slot)
        start(stores(b, p, slot))
        return carry

    lax.fori_loop(lo, hi, body, 0)

    @pl.when(jnp.logical_and(hi > lo, p * n_live + hi == t_end))
    def _():
        @pl.when(t_end >= 2)
        def _():
            wait(stores(hi - 1, p, t_end & 1))
        wait(stores(hi - 1, p, (t_end - 1) & 1))


def _zero_blocks(lo, hi, zbuf, store):
    @pl.when(hi > lo)
    def _():
        zbuf[...] = jnp.zeros_like(zbuf)

        def body(b, carry):
            cp = store(b)
            cp.start()
            cp.wait()
            return carry

        lax.fori_loop(lo, hi, body, 0)


def _up_kernel(lo_ref, hi_ref, nb_ref, x_hbm, w1_ref, w3_ref, g_hbm, xbuf, obuf, zbuf, w1b, w3b, in_sem, out_sem, z_sem):
    j = pl.program_id(0)
    e = pl.program_id(1)
    last_e = pl.num_programs(1) - 1
    tb, tf = obuf.shape[1:]
    w1b[...] = w1_ref[...].astype(BF16)
    w3b[...] = w3_ref[...].astype(BF16)

    def rows(b):
        return pl.ds(pl.multiple_of(b * tb, tb), tb)

    def cols(p):
        return pl.ds(pl.multiple_of(p * tf, tf), tf)

    def loads(b, p, slot):
        return (pltpu.make_async_copy(x_hbm.at[rows(b), :], xbuf.at[slot], in_sem.at[slot]),)

    def stores(b, p, slot):
        return (pltpu.make_async_copy(obuf.at[slot], g_hbm.at[rows(b), cols(p)], out_sem.at[slot]),)

    def compute(b, slot):
        if xbuf.dtype == U32:
            x = jnp.concatenate(_unpack_bf16_pairs(xbuf[slot]), axis=1)
        else:
            x = xbuf[slot]
        a = jnp.dot(x, w1b[...], preferred_element_type=F32)
        c = jnp.dot(x, w3b[...], preferred_element_type=F32)
        obuf[slot] = (_silu(a) * c).astype(obuf.dtype)

    _sweep_blocks(lo_ref[e], hi_ref[e], hi_ref[last_e], j, pl.num_programs(0), loads, compute, stores)

    @pl.when(e == last_e)
    def _():
        _zero_blocks(hi_ref[e], nb_ref[0], zbuf,
                     lambda b: pltpu.make_async_copy(zbuf, g_hbm.at[rows(b), cols(j)], z_sem.at[0]))


def _swiglu_up(xb, blk_lo, blk_hi, w1, w3, layer, tb):
    rows, xw = xb.shape
    n_exp, d, f = w1.shape[1:]
    wb, xbytes = w1.dtype.itemsize, xb.dtype.itemsize
    tf = _fit_tile(f, (1024, 512, 256, 128),
                   lambda t: 4 * d * t * wb + 4 * d * t + 2 * tb * xw * xbytes + 4 * tb * t + 2 * tb * d + 16 * tb * t)
    nb = jnp.full((1,), rows // tb, I32)
    w_spec = pl.BlockSpec((None, None, d, tf), lambda j, e, lo, hi, nb: (layer, e, 0, j))
    return pl.pallas_call(
        _up_kernel,
        out_shape=jax.ShapeDtypeStruct((rows, f), BF16),
        grid_spec=pltpu.PrefetchScalarGridSpec(
            num_scalar_prefetch=3,
            grid=(f // tf, n_exp),
            in_specs=[pl.BlockSpec(memory_space=pl.ANY), w_spec, w_spec],
            out_specs=pl.BlockSpec(memory_space=pl.ANY),
            scratch_shapes=[pltpu.VMEM((2, tb, xw), xb.dtype), pltpu.VMEM((2, tb, tf), BF16),
                            pltpu.VMEM((tb, tf), BF16), pltpu.VMEM((d, tf), BF16), pltpu.VMEM((d, tf), BF16),
                            pltpu.SemaphoreType.DMA((2,)), pltpu.SemaphoreType.DMA((2,)),
                            pltpu.SemaphoreType.DMA((1,))],
        ),
        compiler_params=_params(("arbitrary", "arbitrary")),
        name="swiglu_up",
    )(blk_lo, blk_hi, nb, xb, w1, w3)


def _down_kernel(lo_ref, hi_ref, nb_ref, g_hbm, w2_ref, *rest, n_lat):
    n = pl.program_id(0)
    e = pl.program_id(1)
    last_e = pl.num_programs(1) - 1
    resid = len(rest) == 11
    if resid:
        x_hbm, gt_ref, y_hbm, gbuf, obuf, zbuf, w2b, in_sem, out_sem, z_sem, xbuf = rest
    else:
        y_hbm, gbuf, obuf, zbuf, w2b, in_sem, out_sem, z_sem = rest
    tb, tn = obuf.shape[1:]
    w2b[...] = w2_ref[...].astype(BF16)

    def rows(b):
        return pl.ds(pl.multiple_of(b * tb, tb), tb)

    def cols(p):
        return pl.ds(pl.multiple_of(p * tn, tn), tn)

    def loads(b, p, slot):
        cps = [pltpu.make_async_copy(g_hbm.at[rows(b), :], gbuf.at[slot], in_sem.at[0, slot])]
        if resid:
            cps.append(pltpu.make_async_copy(x_hbm.at[rows(b), cols(p)], xbuf.at[slot], in_sem.at[1, slot]))
        return cps

    def stores(b, p, slot):
        return (pltpu.make_async_copy(obuf.at[slot], y_hbm.at[rows(b), cols(p)], out_sem.at[slot]),)

    def compute(b, slot):
        y = jnp.dot(gbuf[slot], w2b[...], preferred_element_type=F32)
        if resid:
            y = xbuf[slot] + _row_select(gt_ref, b * tb, tb, n_lat) * y
        obuf[slot] = y

    _sweep_blocks(lo_ref[e], hi_ref[e], hi_ref[last_e], n, pl.num_programs(0), loads, compute, stores)

    @pl.when(e == last_e)
    def _():
        _zero_blocks(hi_ref[e], nb_ref[0], zbuf,
                     lambda b: pltpu.make_async_copy(zbuf, y_hbm.at[rows(b), cols(n)], z_sem.at[0]))


def _swiglu_down(gact, blk_lo, blk_hi, w2, layer, tb, resid=None):
    rows, f = gact.shape
    n_exp, _, d = w2.shape[1:]
    wb = w2.dtype.itemsize
    tn = _fit_tile(d, (1024, 512, 256),
                   lambda t: 2 * f * t * wb + 2 * f * t + 4 * tb * f + 16 * tb * t + 4 * tb * t)
    nb = jnp.full((1,), rows // tb, I32)
    in_specs = [pl.BlockSpec(memory_space=pl.ANY),
                pl.BlockSpec((None, None, f, tn), lambda n, e, lo, hi, nb: (layer, e, 0, n))]
    args = [gact, w2]
    scratch = [pltpu.VMEM((2, tb, f), gact.dtype), pltpu.VMEM((2, tb, tn), F32), pltpu.VMEM((tb, tn), F32),
               pltpu.VMEM((f, tn), BF16),
               pltpu.SemaphoreType.DMA((2, 2)), pltpu.SemaphoreType.DMA((2,)), pltpu.SemaphoreType.DMA((1,))]
    n_lat = None
    if resid is not None:
        x, gt, n_lat = resid
        in_specs += [pl.BlockSpec(memory_space=pl.ANY),
                     pl.BlockSpec((2, tn), lambda n, e, lo, hi, nb: (0, n))]
        args += [x, gt]
        scratch.append(pltpu.VMEM((2, tb, tn), F32))
    return pl.pallas_call(
        functools.partial(_down_kernel, n_lat=n_lat),
        out_shape=jax.ShapeDtypeStruct((rows, d), F32),
        grid_spec=pltpu.PrefetchScalarGridSpec(
            num_scalar_prefetch=3,
            grid=(d // tn, n_exp),
            in_specs=in_specs,
            out_specs=pl.BlockSpec(memory_space=pl.ANY),
            scratch_shapes=scratch,
        ),
        compiler_params=_params(("arbitrary", "arbitrary")),
        name="swiglu_down",
    )(blk_lo, blk_hi, nb, *args)


def _dense_ffn(x, g2, sh, sc, gt, w1, w3, w2, layer, n_lat):
    r = x.shape[0]
    tb = _pick(r, (640, 256))
    blk_lo = jnp.zeros((1,), I32)
    blk_hi = jnp.full((1,), r // tb, I32)
    h = _ffn_norm(x, g2, sh, sc, n_lat)
    gact = _swiglu_up(h, blk_lo, blk_hi, w1[:, None], w3[:, None], layer, tb)
    return _swiglu_down(gact, blk_lo, blk_hi, w2[:, None], layer, tb, resid=(x, gt, n_lat))


def _route_kernel(x_ref, g_ref, sh_ref, sc_ref, wr_ref, br_ref, mi_ref, mf_ref, cnt_ref, h_s, run_s, gs_s, sh_s,
                  *, n_lat, n_exp):
    i = pl.program_id(0)
    tm = x_ref.shape[0]

    @pl.when(i == 0)
    def _():
        run_s[...] = jnp.zeros_like(run_s)

    _norm_mod_rows(x_ref, g_ref, sh_ref, sc_ref, h_s, gs_s, sh_s, i * tm, n_lat)
    h = h_s[...]
    w = wr_ref[...]
    h_hi = h.astype(BF16)
    h_lo = (h - h_hi.astype(F32)).astype(BF16)
    w_hi = w.astype(BF16)
    w_lo = (w - w_hi.astype(F32)).astype(BF16)
    logits = (jnp.dot(h_hi, w_hi, preferred_element_type=F32) + jnp.dot(h_hi, w_lo, preferred_element_type=F32)
              + jnp.dot(h_lo, w_hi, preferred_element_type=F32)) + br_ref[...]
    lane = lax.broadcasted_iota(I32, logits.shape, 1).astype(F32)
    l1 = jnp.where(lane < n_exp, logits, -jnp.inf)
    v1 = l1.max(axis=1, keepdims=True)
    e1 = jnp.where(l1 == v1, lane, float(V7X_LANES)).min(axis=1, keepdims=True)
    l2 = jnp.where(lane == e1, -jnp.inf, l1)
    v2 = l2.max(axis=1, keepdims=True)
    e2 = jnp.where(l2 == v2, lane, float(V7X_LANES)).min(axis=1, keepdims=True)
    t = jnp.exp(v2 - v1)
    g1 = 1.0 / (1.0 + t)
    g2 = t / (1.0 + t)

    onehot = jnp.where(jnp.logical_or(lane == e1, lane == e2), 1.0, 0.0)
    below = lax.broadcasted_iota(I32, (tm, tm), 0) > lax.broadcasted_iota(I32, (tm, tm), 1)
    before = jnp.dot(jnp.where(below, 1.0, 0.0).astype(BF16), onehot.astype(BF16),
                     preferred_element_type=F32) + run_s[...]
    r1 = jnp.where(lane == e1, before, 0.0).sum(axis=1, keepdims=True)
    r2 = jnp.where(lane == e2, before, 0.0).sum(axis=1, keepdims=True)
    run_s[...] = run_s[...] + onehot.sum(axis=0, keepdims=True)

    meta = jnp.where(lane == 0, e1, jnp.where(lane == 1, e2, jnp.where(lane == 2, r1, jnp.where(lane == 3, r2, 0.0))))
    mi_ref[...] = meta.astype(I32)
    mf_ref[...] = jnp.where(lane == 0, g1, jnp.where(lane == 1, g2, 0.0))
    cnt_ref[...] = jnp.broadcast_to(run_s[...], cnt_ref.shape)


def _route(x, g2, sh, sc, wr_pad, br_pad, n_lat, n_exp):
    r, d = x.shape
    tm = ROW_TILE
    return pl.pallas_call(
        functools.partial(_route_kernel, n_lat=n_lat, n_exp=n_exp),
        out_shape=(jax.ShapeDtypeStruct((r, V7X_LANES), I32),
                   jax.ShapeDtypeStruct((r, V7X_LANES), F32),
                   jax.ShapeDtypeStruct((V7X_SUBLANES, V7X_LANES), F32)),
        grid=(r // tm,),
        in_specs=[pl.BlockSpec((tm, d), lambda i: (i, 0)),
                  pl.BlockSpec((1, d), lambda i: (0, 0)),
                  pl.BlockSpec((2, d), lambda i: (0, 0)),
                  pl.BlockSpec((2, d), lambda i: (0, 0)),
                  pl.BlockSpec((d, V7X_LANES), lambda i: (0, 0)),
                  pl.BlockSpec((1, V7X_LANES), lambda i: (0, 0))],
        out_specs=(pl.BlockSpec((tm, V7X_LANES), lambda i: (i, 0)),
                   pl.BlockSpec((tm, V7X_LANES), lambda i: (i, 0)),
                   pl.BlockSpec((V7X_SUBLANES, V7X_LANES), lambda i: (0, 0))),
        scratch_shapes=[pltpu.VMEM((tm, d), F32), pltpu.VMEM((1, V7X_LANES), F32)] + _norm_scratch(d),
        compiler_params=_params(("arbitrary",)),
        name="route",
    )(x, g2, sh, sc, wr_pad, br_pad)


def _row_copy(src, s_row, dst, d_row, sem):
    return pltpu.make_async_copy(src.at[pl.ds(s_row, 1), :], dst.at[pl.ds(d_row, 1), :], sem)


def _dispatch_kernel(d1_ref, d2_ref, x_ref, g_ref, sh_ref, sc_ref, xb_in_ref, xb_ref, h_s, pk_s, sem, gs_s, sh_s,
                     *, n_lat):
    del xb_in_ref
    i = pl.program_id(0)
    tm, d = x_ref.shape
    _norm_mod_rows(x_ref, g_ref, sh_ref, sc_ref, h_s, gs_s, sh_s, i * tm, n_lat)
    pk_s[...] = _pack_bf16_pairs(h_s[:, :d // 2], h_s[:, d // 2:])

    def start(r, c):
        _row_copy(pk_s, r, xb_ref, d1_ref[0, r], sem.at[0]).start(priority=0)
        _row_copy(pk_s, r, xb_ref, d2_ref[0, r], sem.at[1]).start(priority=1)
        return c

    def wait(r, c):
        _row_copy(pk_s, r, xb_ref, d1_ref[0, r], sem.at[0]).wait()
        _row_copy(pk_s, r, xb_ref, d2_ref[0, r], sem.at[1]).wait()
        return c

    lax.fori_loop(0, tm, start, 0, unroll=ROW_DMA_UNROLL)
    lax.fori_loop(0, tm, wait, 0, unroll=ROW_DMA_UNROLL)


def _dispatch(x, g2, sh, sc, d1, d2, n_rows, n_lat):
    r, d = x.shape
    tm = ROW_TILE
    nt = r // tm
    smem_rows = pl.BlockSpec((None, 1, tm), lambda i: (i, 0, 0), memory_space=pltpu.SMEM)
    return pl.pallas_call(
        functools.partial(_dispatch_kernel, n_lat=n_lat),
        out_shape=jax.ShapeDtypeStruct((n_rows, d // 2), U32),
        grid=(nt,),
        in_specs=[smem_rows, smem_rows,
                  pl.BlockSpec((tm, d), lambda i: (i, 0)),
                  pl.BlockSpec((1, d), lambda i: (0, 0)),
                  pl.BlockSpec((2, d), lambda i: (0, 0)),
                  pl.BlockSpec((2, d), lambda i: (0, 0)),
                  pl.BlockSpec(memory_space=pl.ANY)],
        out_specs=pl.BlockSpec(memory_space=pl.ANY),
        scratch_shapes=[pltpu.VMEM((tm, d), F32), pltpu.VMEM((tm, d // 2), U32),
                        pltpu.SemaphoreType.DMA((2,))] + _norm_scratch(d),
        input_output_aliases={6: 0},
        compiler_params=_params(("arbitrary",)),
        name="moe_dispatch",
    )(d1.reshape(nt, 1, tm), d2.reshape(nt, 1, tm), x, g2, sh, sc, jnp.zeros((n_rows, d // 2), U32))


def _combine_kernel(d1_ref, d2_ref, mf_ref, x_ref, gt_ref, yb_ref, o_ref, buf, sem, *, n_lat):
    i = pl.program_id(0)
    tm = x_ref.shape[0]

    def start(r, c):
        _row_copy(yb_ref, d1_ref[0, r], buf.at[0], r, sem.at[0]).start(priority=0)
        _row_copy(yb_ref, d2_ref[0, r], buf.at[1], r, sem.at[1]).start(priority=1)
        return c

    def wait(r, c):
        _row_copy(yb_ref, d1_ref[0, r], buf.at[0], r, sem.at[0]).wait()
        _row_copy(yb_ref, d2_ref[0, r], buf.at[1], r, sem.at[1]).wait()
        return c

    lax.fori_loop(0, tm, start, 0, unroll=ROW_DMA_UNROLL)
    lax.fori_loop(0, tm, wait, 0, unroll=ROW_DMA_UNROLL)
    f = mf_ref[:, 0:1] * buf[0] + mf_ref[:, 1:2] * buf[1]
    o_ref[...] = x_ref[...] + _row_select(gt_ref, i * tm, tm, n_lat) * f


def _combine(x, gt, yb, d1, d2, mf, n_lat, n_out):
    r, d = x.shape
    tm = ROW_TILE
    nt = r // tm
    smem_rows = pl.BlockSpec((None, 1, tm), lambda i: (i, 0, 0), memory_space=pltpu.SMEM)
    return pl.pallas_call(
        functools.partial(_combine_kernel, n_lat=n_lat),
        out_shape=jax.ShapeDtypeStruct((n_out, d), F32),
        grid=(n_out // tm,),
        in_specs=[smem_rows, smem_rows,
                  pl.BlockSpec((tm, V7X_LANES), lambda i: (i, 0)),
                  pl.BlockSpec((tm, d), lambda i: (i, 0)),
                  pl.BlockSpec((2, d), lambda i: (0, 0)),
                  pl.BlockSpec(memory_space=pl.ANY)],
        out_specs=pl.BlockSpec((tm, d), lambda i: (i, 0)),
        scratch_shapes=[pltpu.VMEM((2, tm, d), F32), pltpu.SemaphoreType.DMA((2,))],
        compiler_params=_params(("arbitrary",)),
        name="moe_combine",
    )(d1.reshape(nt, 1, tm), d2.reshape(nt, 1, tm), mf, x, gt, yb)


def _moe_ffn(x, g2, sh, sc, gt, wr, br, w1, w3, w2, layer, n_lat, n_out):
    r, d = x.shape
    n_exp = wr.shape[1]
    blk = MOE_BLOCK
    wr_pad = jnp.zeros((d, V7X_LANES), F32).at[:, :n_exp].set(wr)
    br_pad = jnp.zeros((1, V7X_LANES), F32).at[0, :n_exp].set(br)
    mi, mf, cnt = _route(x, g2, sh, sc, wr_pad, br_pad, n_lat, n_exp)

    counts = cnt[0, :n_exp].astype(I32)
    padded = (counts + blk - 1) // blk * blk
    pad_end = jnp.cumsum(padded)
    pad_start = pad_end - padded
    d1 = pad_start[mi[:, 0]] + mi[:, 2]
    d2 = pad_start[mi[:, 1]] + mi[:, 3]
    n_rows = -(-(TOP_K * r + n_exp * (blk - 1)) // blk) * blk
    blk_lo = (pad_start // blk).astype(I32)
    blk_hi = (pad_end // blk).astype(I32)

    xb = _dispatch(x, g2, sh, sc, d1, d2, n_rows, n_lat)
    gact = _swiglu_up(xb, blk_lo, blk_hi, w1, w3, layer, blk)
    yb = _swiglu_down(gact, blk_lo, blk_hi, w2, layer, blk)
    return _combine(x, gt, yb, d1, d2, mf, n_lat, n_out)


def _rope_tables(n_lat, n_ctx):
    n_rows = n_lat // GRID_W
    axis_dim = HEAD_DIM // 2
    inv_freq = ROPE_THETA ** (-jnp.arange(0, axis_dim, 2, dtype=F32) / axis_dim)

    def axis_tables(n):
        ang = jnp.arange(n, dtype=F32)[:, None] * inv_freq[None, :]
        return jnp.cos(ang), jnp.sin(ang)

    per_row = lambda t: jnp.repeat(t, GRID_W, axis=0)
    per_col = lambda t: jnp.tile(t, (n_rows, 1))
    cr, sr = map(per_row, axis_tables(n_rows))
    cc, sc = map(per_col, axis_tables(GRID_W))
    cos_t = jnp.concatenate([cr, cr, cc, cc], axis=1)
    sin_t = jnp.concatenate([-sr, sr, -sc, sc], axis=1)
    cos_t = jnp.concatenate([cos_t, jnp.ones((n_ctx, HEAD_DIM), F32)], axis=0)
    sin_t = jnp.concatenate([sin_t, jnp.zeros((n_ctx, HEAD_DIM), F32)], axis=0)
    return cos_t, sin_t


def kernel(x, c, ctx, c_ctx, w_ada, b_ada, g_norm1, w_in, conv_w, conv_b, conv_ln_g, conv_ln_b, conv_pw, conv_pw_b, sgu_ln_g, sgu_ln_b, sgu_w, sgu_b, swa_q_g, swa_k_g, swa_sink, glb_q_g, glb_k_g, g_branch, w_out, g_norm2, ffn_w1, ffn_w3, ffn_w2, router_w, router_b, exp_w1, exp_w3, exp_w2):
    batch, n_lat, d = x.shape
    n_ctx = ctx.shape[1]
    depth = w_ada.shape[0]
    gw = d // N_GROUPS
    assert batch == 1 and n_lat % n_ctx == 0 and n_ctx % ROW_TILE == 0 and n_lat % GRID_W == 0
    assert conv_w.shape[1] // 2 < CONV_HALO and w_in.shape[2] == 4 * 2 * gw

    xs = jnp.concatenate([x[0], ctx[0]], axis=0)
    mods = _ada(jnp.stack([c[0], c_ctx], axis=1), w_ada, b_ada)
    cos_t, sin_t = _rope_tables(n_lat, n_ctx)
    row2 = lambda v: v.reshape(1, -1)

    for l in range(depth):
        sh1, sc1, gt1, sh2, sc2, gt2 = (mods[l, :, k * d:(k + 1) * d] for k in range(6))

        p = _inproj(xs, row2(g_norm1[l]), sh1, sc1, w_in[l].astype(BF16), n_lat)
        y_conv = _conv_group(p, conv_w[l], row2(conv_b[l]), row2(conv_ln_g[l]), row2(conv_ln_b[l]),
                             conv_pw[l].astype(BF16), row2(conv_pw_b[l]), n_lat)
        y_sgu = _sgu_group(p, row2(sgu_ln_g[l]), row2(sgu_ln_b[l]), sgu_w[l].astype(BF16), sgu_b[l].T)
        qt_s, k_s, vt_s, _ = _prep(p, row2(swa_q_g[l]), row2(swa_k_g[l]), cos_t, sin_t, block=2)
        y_swa = _swa(qt_s, k_s, vt_s, swa_sink[l], n_lat)
        qt_g, k_g, vt_g, kn = _prep(p, row2(glb_q_g[l]), row2(glb_k_g[l]), cos_t, sin_t, block=3)
        kmax = jnp.sqrt(jnp.max(kn[:, 0, :N_KV_HEADS], axis=0)) * KEY_NORM_MARGIN
        y_glb = _glb(qt_g, k_g, vt_g, kmax, n_lat)
        xs = _outproj((y_conv, y_sgu, y_swa, y_glb), row2(g_branch[l]), w_out[l].astype(BF16), xs, gt1, n_lat)

        if l % 2 == 0:
            xs = _dense_ffn(xs, row2(g_norm2[l]), sh2, sc2, gt2, ffn_w1, ffn_w3, ffn_w2, l // 2, n_lat)
        else:
            xs = _moe_ffn(xs, row2(g_norm2[l]), sh2, sc2, gt2, router_w[l // 2], router_b[l // 2],
                          exp_w1, exp_w3, exp_w2, l // 2, n_lat, n_out=n_lat if l == depth - 1 else n_lat + n_ctx)
    return xs[:n_lat][None]
```

```python
import functools
import math

import jax
import jax.numpy as jnp
from jax import lax
from jax.experimental import pallas as pl
from jax.experimental.pallas import tpu as pltpu

F32 = jnp.float32
BF16 = jnp.bfloat16
I32 = jnp.int32

HEAD_DIM = 128
N_GROUPS = 4
N_KV_HEADS = 2
GRID_W = 64
WINDOW = 128
ROPE_THETA = 10000.0
TOP_K = 2
MOE_BLOCK = 512
EPS = 1e-6
NEG = -1e30
SCALE = HEAD_DIM ** -0.5
Q_PRESCALE = SCALE * math.log2(math.e)
EXP2_SAFE_SHIFT = 60.0
KEY_NORM_MARGIN = 1.01

V7X_VMEM_BYTES = 64 * 1024 * 1024
V7X_LANES = 128
V7X_SUBLANES = 8
VMEM_BUDGET = V7X_VMEM_BYTES - 8 * 1024 * 1024

ROW_TILE = 256
CONV_HALO = 16
NORM_CHUNK = 16
NORM_UNROLL = 8
ROW_DMA_UNROLL = 8


def _pick(n, cands):
    for c in cands:
        if n % c == 0:
            return c
    raise ValueError(f"no tile in {cands} divides {n}")


def _params(sem, vmem=VMEM_BUDGET):
    return pltpu.CompilerParams(dimension_semantics=sem, vmem_limit_bytes=vmem)


def _sigmoid(x):
    return 1.0 / (1.0 + jnp.exp(-x))


def _silu(x):
    return x * _sigmoid(x)


def _gelu_tanh(x):
    c = math.sqrt(2.0 / math.pi)
    return 0.5 * x * (1.0 + jnp.tanh(c * (x + 0.044715 * (x * x * x))))


def _rms(x):
    return x * lax.rsqrt(jnp.mean(x * x, axis=-1, keepdims=True) + EPS)


def _layer_norm(x, g, b):
    xc = x - jnp.mean(x, axis=-1, keepdims=True)
    return xc * lax.rsqrt(jnp.mean(xc * xc, axis=-1, keepdims=True) + EPS) * g + b


def _row_select(vec2_ref, row0, tm, n_lat):
    rows = row0 + lax.broadcasted_iota(I32, (tm, 1), 0)
    return jnp.where(rows >= n_lat, vec2_ref[1:2, :], vec2_ref[0:1, :])


def _norm_mod(x, g, shift, scale):
    return _rms(x) * g * (1.0 + scale) + shift


def _norm_scratch(d):
    return [pltpu.VMEM((2 * V7X_SUBLANES, d), F32), pltpu.VMEM((2 * V7X_SUBLANES, d), F32)]


def _norm_mod_rows(x_ref, g_ref, sh_ref, sc_ref, out_ref, gs_s, sh_s, row0, n_lat):
    tm, d = x_ref.shape
    sub = V7X_SUBLANES
    for t in range(2):
        gs_s[t * sub:(t + 1) * sub, :] = jnp.broadcast_to(g_ref[...] * (1.0 + sc_ref[t:t + 1, :]), (sub, d))
        sh_s[t * sub:(t + 1) * sub, :] = jnp.broadcast_to(sh_ref[t:t + 1, :], (sub, d))

    def body(c, carry):
        r0 = pl.multiple_of(c * NORM_CHUNK, NORM_CHUNK)
        t0 = pl.multiple_of(jnp.where(row0 + r0 >= n_lat, sub, 0), sub)
        x = x_ref[pl.ds(r0, NORM_CHUNK), :].reshape(NORM_CHUNK // sub, sub, d)
        y = _rms(x) * gs_s[pl.ds(t0, sub), :] + sh_s[pl.ds(t0, sub), :]
        out_ref[pl.ds(r0, NORM_CHUNK), :] = y.reshape(NORM_CHUNK, d).astype(out_ref.dtype)
        return carry

    lax.fori_loop(0, tm // NORM_CHUNK, body, 0, unroll=NORM_UNROLL)


def _ada_kernel(s_ref, w_ref, b_ref, o_ref, act_s):
    d, tn = w_ref.shape
    kc = 64
    act_s[...] = _silu(s_ref[...])

    def body(c, acc):
        a0, a1 = acc
        k0 = pl.multiple_of(c * kc, kc)
        w = w_ref[pl.ds(k0, kc), :]
        s = act_s[pl.ds(k0, kc), :]
        a0 = a0 + (w * s[:, 0:1]).reshape(kc // V7X_SUBLANES, V7X_SUBLANES, tn).sum(axis=0)
        a1 = a1 + (w * s[:, 1:2]).reshape(kc // V7X_SUBLANES, V7X_SUBLANES, tn).sum(axis=0)
        return a0, a1

    z = jnp.zeros((V7X_SUBLANES, tn), F32)
    a0, a1 = lax.fori_loop(0, d // kc, body, (z, z), unroll=4)
    o_ref[0:1, :] = a0.sum(axis=0, keepdims=True) + b_ref[...]
    o_ref[1:2, :] = a1.sum(axis=0, keepdims=True) + b_ref[...]


def _ada(cond, w_ada, b_ada):
    depth, d, n = w_ada.shape
    tn = _pick(n, (1024, 512, 256, 128))
    return pl.pallas_call(
        _ada_kernel,
        out_shape=jax.ShapeDtypeStruct((depth, 2, n), F32),
        grid=(depth, n // tn),
        in_specs=[
            pl.BlockSpec((d, 2), lambda l, j: (0, 0)),
            pl.BlockSpec((None, d, tn), lambda l, j: (l, 0, j)),
            pl.BlockSpec((None, 1, tn), lambda l, j: (l, 0, j)),
        ],
        out_specs=pl.BlockSpec((None, 2, tn), lambda l, j: (l, 0, j)),
        scratch_shapes=[pltpu.VMEM((d, 2), F32)],
        compiler_params=_params(("parallel", "parallel")),
        name="ada",
    )(cond, w_ada, b_ada.reshape(depth, 1, n))


def _weight_spec(k, n, itemsize, other_bytes):
    if other_bytes + k * n * itemsize <= VMEM_BUDGET:
        return n, pl.BlockSpec((k, n), lambda i, j: (0, 0), pipeline_mode=pl.Buffered(1))
    tn = _pick(n, (1024, 512, 256))
    return tn, pl.BlockSpec((k, tn), lambda i, j: (0, j))


def _inproj_kernel(x_ref, g_ref, sh_ref, sc_ref, w_ref, o_ref, h_s, gs_s, sh_s, *, n_lat):
    i = pl.program_id(0)
    tm = x_ref.shape[0]

    @pl.when(pl.program_id(1) == 0)
    def _():
        _norm_mod_rows(x_ref, g_ref, sh_ref, sc_ref, h_s, gs_s, sh_s, i * tm, n_lat)

    o_ref[...] = jnp.dot(h_s[...], w_ref[...], preferred_element_type=F32)


def _inproj(x, g, sh, sc, w, n_lat):
    r, d = x.shape
    n = w.shape[1]
    tm = _pick(r, (640, 256))
    tn, w_spec = _weight_spec(d, n, w.dtype.itemsize, 2 * tm * d * 4 + 2 * tm * n * 4 + tm * d * 2)
    return pl.pallas_call(
        functools.partial(_inproj_kernel, n_lat=n_lat),
        out_shape=jax.ShapeDtypeStruct((r, n), F32),
        grid=(r // tm, n // tn),
        in_specs=[
            pl.BlockSpec((tm, d), lambda i, j: (i, 0)),
            pl.BlockSpec((1, d), lambda i, j: (0, 0)),
            pl.BlockSpec((2, d), lambda i, j: (0, 0)),
            pl.BlockSpec((2, d), lambda i, j: (0, 0)),
            w_spec,
        ],
        out_specs=pl.BlockSpec((tm, tn), lambda i, j: (i, j)),
        scratch_shapes=[pltpu.VMEM((tm, d), BF16)] + _norm_scratch(d),
        compiler_params=_params(("parallel", "arbitrary")),
        name="inproj",
    )(x, g, sh, sc, w)


def _conv_kernel(pm_ref, pp_ref, pn_ref, cw_ref, cb_ref, lg_ref, lb_ref, pw_ref, pb_ref, o_ref, ext_s, sft_s,
                 *, n_lat_tiles, n_tiles):
    i = pl.program_id(0)
    tm, gw = o_ref.shape
    kw = cw_ref.shape[0]

    def glu(p):
        return p[:, :gw] * _sigmoid(p[:, gw:])

    prev_ok = jnp.logical_and(i != 0, i != n_lat_tiles)
    next_ok = jnp.logical_and(i != n_lat_tiles - 1, i != n_tiles - 1)
    ext_s[0:CONV_HALO, :] = jnp.where(prev_ok, glu(pp_ref[...]), 0.0)
    ext_s[CONV_HALO:CONV_HALO + tm, :] = glu(pm_ref[...])
    ext_s[CONV_HALO + tm:, :] = jnp.where(next_ok, glu(pn_ref[...]), 0.0)

    sub = V7X_SUBLANES
    span = sft_s.shape[1]
    for b in range(1, sub):
        sft_s[b - 1] = ext_s[b:b + span, :]

    acc = jnp.zeros((tm, gw), F32) + cb_ref[...]
    for k in range(kw):
        a, b = divmod(CONV_HALO - kw // 2 + k, sub)
        tap = ext_s[a * sub:a * sub + tm, :] if b == 0 else sft_s[b - 1, a * sub:a * sub + tm, :]
        acc = acc + cw_ref[k:k + 1, :] * tap
    y = _silu(_layer_norm(acc, lg_ref[...], lb_ref[...]))
    o_ref[...] = jnp.dot(y.astype(BF16), pw_ref[...], preferred_element_type=F32) + pb_ref[...]


def _conv_group(p, cw, cb, lg, lb, pw, pb, n_lat):
    r = p.shape[0]
    kw, gw = cw.shape
    tm = ROW_TILE
    n_tiles = r // tm
    hb = tm // CONV_HALO
    last_hb = r // CONV_HALO - 1
    return pl.pallas_call(
        functools.partial(_conv_kernel, n_lat_tiles=n_lat // tm, n_tiles=n_tiles),
        out_shape=jax.ShapeDtypeStruct((r, gw), F32),
        grid=(n_tiles,),
        in_specs=[
            pl.BlockSpec((tm, 2 * gw), lambda i: (i, 0)),
            pl.BlockSpec((CONV_HALO, 2 * gw), lambda i: (jnp.maximum(i * hb - 1, 0), 0)),
            pl.BlockSpec((CONV_HALO, 2 * gw), lambda i: (jnp.minimum((i + 1) * hb, last_hb), 0)),
            pl.BlockSpec((kw, gw), lambda i: (0, 0)),
            pl.BlockSpec((1, gw), lambda i: (0, 0)),
            pl.BlockSpec((1, gw), lambda i: (0, 0)),
            pl.BlockSpec((1, gw), lambda i: (0, 0)),
            pl.BlockSpec((gw, gw), lambda i: (0, 0)),
            pl.BlockSpec((1, gw), lambda i: (0, 0)),
        ],
        out_specs=pl.BlockSpec((tm, gw), lambda i: (i, 0)),
        scratch_shapes=[pltpu.VMEM((tm + 2 * CONV_HALO, gw), F32),
                        pltpu.VMEM((V7X_SUBLANES - 1, tm + 2 * CONV_HALO - V7X_SUBLANES, gw), F32)],
        compiler_params=_params(("parallel",)),
        name="conv_group",
    )(p, p, p, cw, cb, lg, lb, pw, pb)


def _sgu_kernel(p_ref, lg_ref, lb_ref, ws_ref, bs_ref, o_ref):
    tm, gw = o_ref.shape
    n_h, ch, _ = ws_ref.shape
    hd = gw // n_h
    z = _gelu_tanh(p_ref[...])
    u = z[:, :gw]
    v = _layer_norm(z[:, gw:], lg_ref[...], lb_ref[...]).astype(BF16)
    for c in range(tm // ch):
        rows = slice(c * ch, (c + 1) * ch)
        parts = []
        for h in range(n_h):
            s = jnp.dot(ws_ref[h], v[rows, h * hd:(h + 1) * hd], preferred_element_type=F32)
            parts.append(s + bs_ref[:, h:h + 1])
        o_ref[rows, :] = u[rows, :] * jnp.concatenate(parts, axis=1)


def _sgu_group(p, lg, lb, ws, bs_t):
    r = p.shape[0]
    gw = lg.shape[1]
    n_h, ch, _ = ws.shape
    tm = ROW_TILE
    return pl.pallas_call(
        _sgu_kernel,
        out_shape=jax.ShapeDtypeStruct((r, gw), F32),
        grid=(r // tm,),
        in_specs=[
            pl.BlockSpec((tm, 2 * gw), lambda i: (i, 1)),
            pl.BlockSpec((1, gw), lambda i: (0, 0)),
            pl.BlockSpec((1, gw), lambda i: (0, 0)),
            pl.BlockSpec((n_h, ch, ch), lambda i: (0, 0, 0)),
            pl.BlockSpec((ch, n_h), lambda i: (0, 0)),
        ],
        out_specs=pl.BlockSpec((tm, gw), lambda i: (i, 0)),
        compiler_params=_params(("parallel",)),
        name="sgu_group",
    )(p, lg, lb, ws, bs_t)


def _prep_kernel(p_ref, qg_ref, kg_ref, cos_ref, sin_ref, qt_ref, k_ref, vt_ref, kn_ref):
    tm = p_ref.shape[0]
    kvw = k_ref.shape[1]
    qw = p_ref.shape[1] - 2 * kvw
    cos = cos_ref[...]
    sin = sin_ref[...]
    lane = lax.broadcasted_iota(I32, (tm, HEAD_DIM), 1)
    low = (lane & (HEAD_DIM // 4)) == 0

    def head(x, g, mult):
        y = _rms(x) * g
        partner = jnp.where(low, pltpu.roll(y, HEAD_DIM - HEAD_DIM // 4, 1), pltpu.roll(y, HEAD_DIM // 4, 1))
        return (y * cos + partner * sin) * mult

    eye = jnp.where(lax.broadcasted_iota(I32, (HEAD_DIM, HEAD_DIM), 0)
                    == lax.broadcasted_iota(I32, (HEAD_DIM, HEAD_DIM), 1), 1.0, 0.0).astype(BF16)

    def put_t(ref, h, val):
        ref[h * HEAD_DIM:(h + 1) * HEAD_DIM, :] = _nt_dot(eye, val.astype(BF16)).astype(BF16)

    for h in range(qw // HEAD_DIM):
        put_t(qt_ref, h, head(p_ref[:, h * HEAD_DIM:(h + 1) * HEAD_DIM], qg_ref[...], Q_PRESCALE))
    kn_lane = lax.broadcasted_iota(I32, kn_ref.shape, 1)
    kn = jnp.zeros(kn_ref.shape, F32)
    for h in range(kvw // HEAD_DIM):
        kh = head(p_ref[:, qw + h * HEAD_DIM:qw + (h + 1) * HEAD_DIM], kg_ref[...], 1.0)
        k_ref[:, h * HEAD_DIM:(h + 1) * HEAD_DIM] = kh.astype(BF16)
        put_t(vt_ref, h, p_ref[:, qw + kvw + h * HEAD_DIM:qw + kvw + (h + 1) * HEAD_DIM])
        ksq = jnp.sum(kh * kh, axis=1, keepdims=True).max(axis=0, keepdims=True)
        kn = jnp.where(kn_lane == h, ksq, kn)
    kn_ref[...] = kn


def _prep(p, qg, kg, cos_t, sin_t, block):
    r = p.shape[0]
    qw = N_GROUPS * HEAD_DIM
    kvw = N_KV_HEADS * HEAD_DIM
    tm = ROW_TILE
    out_shape = (jax.ShapeDtypeStruct((qw, r), BF16), jax.ShapeDtypeStruct((r, kvw), BF16),
                 jax.ShapeDtypeStruct((kvw, r), BF16),
                 jax.ShapeDtypeStruct((r // tm, V7X_SUBLANES, V7X_LANES), F32))
    out_specs = (pl.BlockSpec((qw, tm), lambda i: (0, i)), pl.BlockSpec((tm, kvw), lambda i: (i, 0)),
                 pl.BlockSpec((kvw, tm), lambda i: (0, i)),
                 pl.BlockSpec((None, V7X_SUBLANES, V7X_LANES), lambda i: (i, 0, 0)))
    return pl.pallas_call(
        _prep_kernel,
        out_shape=out_shape,
        grid=(r // tm,),
        in_specs=[
            pl.BlockSpec((tm, qw + 2 * kvw), lambda i: (i, block)),
            pl.BlockSpec((1, HEAD_DIM), lambda i: (0, 0)),
            pl.BlockSpec((1, HEAD_DIM), lambda i: (0, 0)),
            pl.BlockSpec((tm, HEAD_DIM), lambda i: (i, 0)),
            pl.BlockSpec((tm, HEAD_DIM), lambda i: (i, 0)),
        ],
        out_specs=out_specs,
        compiler_params=_params(("parallel",)),
        name="qkv_prep",
    )(p, qg, kg, cos_t, sin_t)


def _nt_dot(a, b):
    return lax.dot_general(a, b, (((1,), (1,)), ((), ())), preferred_element_type=F32)


def _swa_kernel(sink_ref, qt_ref, km_ref, kp_ref, kn_ref, vm_ref, vp_ref, vn_ref, kc_ref, vc_ref, o_ref,
                *, n_lat):
    i = pl.program_id(0)
    tq = qt_ref.shape[1]
    hb = kp_ref.shape[0]
    n_rep = (qt_ref.shape[0] // HEAD_DIM) // N_KV_HEADS
    nk = tq + 2 * hb
    kj = lax.broadcasted_iota(I32, (nk, tq), 0) - hb
    qi = lax.broadcasted_iota(I32, (nk, tq), 1)
    kglob = i * tq + kj
    ok1 = (jnp.abs(kj - qi) <= WINDOW) & (kglob >= 0) & (kglob < n_lat) & (i * tq < n_lat)
    ok = jnp.concatenate([ok1.astype(F32)] * n_rep, axis=1) > 0.5
    q_lane = lax.broadcasted_iota(I32, (1, n_rep * tq), 1)
    for j in range(N_KV_HEADS):
        kv = slice(j * HEAD_DIM, (j + 1) * HEAD_DIM)
        q2t = jnp.concatenate(
            [qt_ref[(j * n_rep + g) * HEAD_DIM:(j * n_rep + g + 1) * HEAD_DIM, :] for g in range(n_rep)], axis=1)
        kw = jnp.concatenate([kp_ref[:, kv], km_ref[:, kv], kn_ref[:, kv]], axis=0)
        vwt = jnp.concatenate([vp_ref[kv, :], vm_ref[kv, :], vn_ref[kv, :]], axis=1)
        s_w = jnp.where(ok, jnp.dot(kw, q2t, preferred_element_type=F32), NEG)
        s_c = jnp.dot(kc_ref[:, kv], q2t, preferred_element_type=F32)
        sk = jnp.zeros((1, n_rep * tq), F32)
        for g in range(n_rep):
            sk = jnp.where(q_lane >= g * tq, sink_ref[j * n_rep + g] * math.log2(math.e), sk)
        m = jnp.maximum(jnp.maximum(s_w.max(axis=0, keepdims=True), s_c.max(axis=0, keepdims=True)), sk)
        p_w = jnp.exp2(s_w - m)
        p_c = jnp.exp2(s_c - m)
        den = jnp.exp2(sk - m) + p_w.sum(axis=0, keepdims=True) + p_c.sum(axis=0, keepdims=True)
        ot = (jnp.dot(vwt, p_w.astype(BF16), preferred_element_type=F32)
              + jnp.dot(vc_ref[kv, :], p_c.astype(BF16), preferred_element_type=F32)) / den
        o = ot.T
        for g in range(n_rep):
            h = j * n_rep + g
            o_ref[:, h * HEAD_DIM:(h + 1) * HEAD_DIM] = o[g * tq:(g + 1) * tq, :]


def _swa(qt, k, vt, sink, n_lat):
    qw = qt.shape[0]
    r, kvw = k.shape
    n_ctx = r - n_lat
    tq = ROW_TILE
    hb = WINDOW
    per = tq // hb
    last_hb = r // hb - 1
    prev_b = lambda i: jnp.maximum(i * per - 1, 0)
    next_b = lambda i: jnp.minimum((i + 1) * per, last_hb)
    return pl.pallas_call(
        functools.partial(_swa_kernel, n_lat=n_lat),
        out_shape=jax.ShapeDtypeStruct((r, qw), F32),
        grid=(r // tq,),
        in_specs=[
            pl.BlockSpec(memory_space=pltpu.SMEM),
            pl.BlockSpec((qw, tq), lambda i: (0, i)),
            pl.BlockSpec((tq, kvw), lambda i: (i, 0)),
            pl.BlockSpec((hb, kvw), lambda i: (prev_b(i), 0)),
            pl.BlockSpec((hb, kvw), lambda i: (next_b(i), 0)),
            pl.BlockSpec((kvw, tq), lambda i: (0, i)),
            pl.BlockSpec((kvw, hb), lambda i: (0, prev_b(i))),
            pl.BlockSpec((kvw, hb), lambda i: (0, next_b(i))),
            pl.BlockSpec((n_ctx, kvw), lambda i: (n_lat // n_ctx, 0)),
            pl.BlockSpec((kvw, n_ctx), lambda i: (0, n_lat // n_ctx)),
        ],
        out_specs=pl.BlockSpec((tq, qw), lambda i: (i, 0)),
        compiler_params=_params(("parallel",)),
        name="window_attn",
    )(sink, qt, k, k, k, vt, vt, vt, k, vt)


def _glb_kernel(kmax_ref, qt_ref, k_ref, vt_ref, o_ref, acc_s, l_s, *, n_lat, tk, unroll):
    j = pl.program_id(0)
    i = pl.program_id(1)
    tq = qt_ref.shape[1]
    n_rep = qt_ref.shape[0] // HEAD_DIM
    nq = n_rep * tq
    n_ctx = k_ref.shape[0] - n_lat
    q2t = jnp.concatenate([qt_ref[g * HEAD_DIM:(g + 1) * HEAD_DIM, :] for g in range(n_rep)], axis=1)
    qf = q2t.astype(F32)
    bound = jnp.sqrt(jnp.sum(qf * qf, axis=0, keepdims=True)) * kmax_ref[j]

    def scores(k0, size):
        return jnp.dot(k_ref[pl.ds(k0, size), :], q2t, preferred_element_type=F32)

    def shifted_by_bound():
        def add(k0, size):
            p = jnp.exp2(scores(k0, size) - bound)
            l_s[...] += p.sum(axis=0, keepdims=True)
            acc_s[...] += jnp.dot(vt_ref[:, pl.ds(k0, size)], p.astype(BF16), preferred_element_type=F32)

        acc_s[...] = jnp.zeros_like(acc_s)
        l_s[...] = jnp.zeros_like(l_s)
        add(n_lat, n_ctx)

        @pl.when(i * tq < n_lat)
        def _():
            def body(c, carry):
                for u in range(unroll):
                    add(pl.multiple_of((c * unroll + u) * tk, tk), tk)
                return carry

            lax.fori_loop(0, n_lat // tk // unroll, body, 0)

    def shifted_by_running_max():
        def update(s, vt, m, l):
            m_new = jnp.maximum(m, s.max(axis=0, keepdims=True))
            alpha = jnp.exp2(m - m_new)
            p = jnp.exp2(s - m_new)
            l = alpha * l + p.sum(axis=0, keepdims=True)
            acc_s[...] = alpha * acc_s[...] + jnp.dot(vt, p.astype(BF16), preferred_element_type=F32)
            return m_new, l

        acc_s[...] = jnp.zeros_like(acc_s)
        first = update(scores(n_lat, n_ctx), vt_ref[:, n_lat:],
                       jnp.full((1, nq), NEG, F32), jnp.zeros((1, nq), F32))

        def body(c, carry):
            k0 = pl.multiple_of(c * tk, tk)
            return update(scores(k0, tk), vt_ref[:, pl.ds(k0, tk)], *carry)

        _, l = lax.fori_loop(0, jnp.where(i * tq < n_lat, n_lat // tk, 0), body, first)
        l_s[...] = l

    lax.cond(jnp.max(bound) <= EXP2_SAFE_SHIFT, shifted_by_bound, shifted_by_running_max)
    o = (acc_s[...] / l_s[...]).T
    for g in range(n_rep):
        o_ref[:, g * HEAD_DIM:(g + 1) * HEAD_DIM] = o[g * tq:(g + 1) * tq, :]


def _glb(qt, k, vt, kmax, n_lat):
    r, kvw = k.shape
    qw = qt.shape[0]
    tq = ROW_TILE
    gq = qw // N_KV_HEADS
    tk = _pick(n_lat, (4096, 2048, 1024, 512, 256))
    unroll = 2 if (n_lat // tk) % 2 == 0 else 1
    nq = gq // HEAD_DIM * tq
    return pl.pallas_call(
        functools.partial(_glb_kernel, n_lat=n_lat, tk=tk, unroll=unroll),
        out_shape=jax.ShapeDtypeStruct((r, qw), F32),
        grid=(N_KV_HEADS, r // tq),
        in_specs=[
            pl.BlockSpec(memory_space=pltpu.SMEM),
            pl.BlockSpec((gq, tq), lambda j, i: (j, i)),
            pl.BlockSpec((r, HEAD_DIM), lambda j, i: (0, j)),
            pl.BlockSpec((HEAD_DIM, r), lambda j, i: (j, 0)),
        ],
        out_specs=pl.BlockSpec((tq, gq), lambda j, i: (i, j)),
        scratch_shapes=[pltpu.VMEM((HEAD_DIM, nq), F32), pltpu.VMEM((1, nq), F32)],
        compiler_params=_params(("parallel", "parallel")),
        name="global_attn",
    )(kmax, qt, k, vt)


def _outproj_kernel(y0_ref, y1_ref, y2_ref, y3_ref, gb_ref, w_ref, x_ref, gt_ref, o_ref, yn_s, *, n_lat):
    i = pl.program_id(0)
    tm = x_ref.shape[0]

    @pl.when(pl.program_id(1) == 0)
    def _():
        for g, y_ref in enumerate((y0_ref, y1_ref, y2_ref, y3_ref)):
            gw = y_ref.shape[1]
            cols = slice(g * gw, (g + 1) * gw)
            yn_s[:, cols] = (_rms(y_ref[...]) * gb_ref[:, cols]).astype(BF16)

    gate = _row_select(gt_ref, i * tm, tm, n_lat)
    o_ref[...] = x_ref[...] + gate * jnp.dot(yn_s[...], w_ref[...], preferred_element_type=F32)


def _outproj(ys, gb, w, x, gt, n_lat):
    r, d = x.shape
    gw = ys[0].shape[1]
    tm = _pick(r, (640, 256))
    tn, w_spec = _weight_spec(d, d, w.dtype.itemsize, 2 * tm * d * 4 * 3 + tm * d * 2)
    ysp = pl.BlockSpec((tm, gw), lambda i, j: (i, 0))
    return pl.pallas_call(
        functools.partial(_outproj_kernel, n_lat=n_lat),
        out_shape=jax.ShapeDtypeStruct((r, d), F32),
        grid=(r // tm, d // tn),
        in_specs=[ysp, ysp, ysp, ysp,
                  pl.BlockSpec((1, d), lambda i, j: (0, 0)),
                  w_spec,
                  pl.BlockSpec((tm, tn), lambda i, j: (i, j)),
                  pl.BlockSpec((2, tn), lambda i, j: (0, j))],
        out_specs=pl.BlockSpec((tm, tn), lambda i, j: (i, j)),
        scratch_shapes=[pltpu.VMEM((tm, d), BF16)],
        compiler_params=_params(("parallel", "arbitrary")),
        name="outproj",
    )(*ys, gb, w, x, gt)


def _ffn_norm_kernel(x_ref, g_ref, sh_ref, sc_ref, o_ref, gs_s, sh_s, *, n_lat):
    _norm_mod_rows(x_ref, g_ref, sh_ref, sc_ref, o_ref, gs_s, sh_s, pl.program_id(0) * x_ref.shape[0], n_lat)


def _ffn_norm(x, g, sh, sc, n_lat):
    r, d = x.shape
    tm = _pick(r, (640, 256))
    return pl.pallas_call(
        functools.partial(_ffn_norm_kernel, n_lat=n_lat),
        out_shape=jax.ShapeDtypeStruct((r, d), BF16),
        grid=(r // tm,),
        in_specs=[pl.BlockSpec((tm, d), lambda i: (i, 0)),
                  pl.BlockSpec((1, d), lambda i: (0, 0)),
                  pl.BlockSpec((2, d), lambda i: (0, 0)),
                  pl.BlockSpec((2, d), lambda i: (0, 0))],
        out_specs=pl.BlockSpec((tm, d), lambda i: (i, 0)),
        scratch_shapes=_norm_scratch(d),
        compiler_params=_params(("parallel",)),
        name="ffn_norm",
    )(x, g, sh, sc)


U32 = jnp.uint32
BF16_HI_MASK = 0xFFFF0000


def _pack_bf16_pairs(lo, hi):
    lo_bits = pltpu.bitcast(lo.astype(BF16).astype(F32), U32)
    hi_bits = pltpu.bitcast(hi.astype(BF16).astype(F32), U32)
    return hi_bits | (lo_bits >> 16)


def _unpack_bf16_pairs(packed):
    lo = pltpu.bitcast(packed << 16, F32).astype(BF16)
    hi = pltpu.bitcast(packed & U32(BF16_HI_MASK), F32).astype(BF16)
    return lo, hi


def _fit_tile(n, cands, vmem_bytes):
    for c in cands:
        if n % c == 0 and vmem_bytes(c) <= VMEM_BUDGET:
            return c
    raise ValueError(f"no tile in {cands} divides {n} within the VMEM budget")


def _sweep_blocks(lo, hi, n_live, p, n_pass, loads, compute, stores):
    def start(copies, priority=0):
        for cp in copies:
            cp.start(priority=priority)

    def wait(copies):
        for cp in copies:
            cp.wait()

    t_end = n_pass * n_live
    load_priority = 1

    @pl.when(jnp.logical_and(p == 0, jnp.logical_and(lo == 0, hi > 0)))
    def _():
        start(loads(0, 0, 0), load_priority)

    def body(b, carry):
        t = p * n_live + b
        slot = t & 1
        wait(loads(b, p, slot))
        wrap = b + 1 >= n_live

        @pl.when(t + 1 < t_end)
        def _():
            start(loads(jnp.where(wrap, 0, b + 1), jnp.where(wrap, p + 1, p), 1 - slot), load_priority)

        @pl.when(t >= 2)
        def _():
            wait(stores(b, p, slot))
        compute(b, slot)
        start(stores(b, p, slot))
        return carry

    lax.fori_loop(lo, hi, body, 0)

    @pl.when(jnp.logical_and(hi > lo, p * n_live + hi == t_end))
    def _():
        @pl.when(t_end >= 2)
        def _():
            wait(stores(hi - 1, p, t_end & 1))
        wait(stores(hi - 1, p, (t_end - 1) & 1))


def _zero_blocks(lo, hi, zbuf, store):
    @pl.when(hi > lo)
    def _():
        zbuf[...] = jnp.zeros_like(zbuf)

        def body(b, carry):
            cp = store(b)
            cp.start()
            cp.wait()
            return carry

        lax.fori_loop(lo, hi, body, 0)


def _up_kernel(lo_ref, hi_ref, nb_ref, x_hbm, w1_ref, w3_ref, g_hbm, xbuf, obuf, zbuf, w1b, w3b, in_sem, out_sem, z_sem):
    j = pl.program_id(0)
    e = pl.program_id(1)
    last_e = pl.num_programs(1) - 1
    tb, tf = obuf.shape[1:]
    w1b[...] = w1_ref[...].astype(BF16)
    w3b[...] = w3_ref[...].astype(BF16)

    def rows(b):
        return pl.ds(pl.multiple_of(b * tb, tb), tb)

    def cols(p):
        return pl.ds(pl.multiple_of(p * tf, tf), tf)

    def loads(b, p, slot):
        return (pltpu.make_async_copy(x_hbm.at[rows(b), :], xbuf.at[slot], in_sem.at[slot]),)

    def stores(b, p, slot):
        return (pltpu.make_async_copy(obuf.at[slot], g_hbm.at[rows(b), cols(p)], out_sem.at[slot]),)

    def compute(b, slot):
        if xbuf.dtype == U32:
            x = jnp.concatenate(_unpack_bf16_pairs(xbuf[slot]), axis=1)
        else:
            x = xbuf[slot]
        a = jnp.dot(x, w1b[...], preferred_element_type=F32)
        c = jnp.dot(x, w3b[...], preferred_element_type=F32)
        obuf[slot] = (_silu(a) * c).astype(obuf.dtype)

    _sweep_blocks(lo_ref[e], hi_ref[e], hi_ref[last_e], j, pl.num_programs(0), loads, compute, stores)

    @pl.when(e == last_e)
    def _():
        _zero_blocks(hi_ref[e], nb_ref[0], zbuf,
                     lambda b: pltpu.make_async_copy(zbuf, g_hbm.at[rows(b), cols(j)], z_sem.at[0]))


def _swiglu_up(xb, blk_lo, blk_hi, w1, w3, layer, tb):
    rows, xw = xb.shape
    n_exp, d, f = w1.shape[1:]
    wb, xbytes = w1.dtype.itemsize, xb.dtype.itemsize
    tf = _fit_tile(f, (1024, 512, 256, 128),
                   lambda t: 4 * d * t * wb + 4 * d * t + 2 * tb * xw * xbytes + 4 * tb * t + 2 * tb * d + 16 * tb * t)
    nb = jnp.full((1,), rows // tb, I32)
    w_spec = pl.BlockSpec((None, None, d, tf), lambda j, e, lo, hi, nb: (layer, e, 0, j))
    return pl.pallas_call(
        _up_kernel,
        out_shape=jax.ShapeDtypeStruct((rows, f), BF16),
        grid_spec=pltpu.PrefetchScalarGridSpec(
            num_scalar_prefetch=3,
            grid=(f // tf, n_exp),
            in_specs=[pl.BlockSpec(memory_space=pl.ANY), w_spec, w_spec],
            out_specs=pl.BlockSpec(memory_space=pl.ANY),
            scratch_shapes=[pltpu.VMEM((2, tb, xw), xb.dtype), pltpu.VMEM((2, tb, tf), BF16),
                            pltpu.VMEM((tb, tf), BF16), pltpu.VMEM((d, tf), BF16), pltpu.VMEM((d, tf), BF16),
                            pltpu.SemaphoreType.DMA((2,)), pltpu.SemaphoreType.DMA((2,)),
                            pltpu.SemaphoreType.DMA((1,))],
        ),
        compiler_params=_params(("arbitrary", "arbitrary")),
        name="swiglu_up",
    )(blk_lo, blk_hi, nb, xb, w1, w3)


def _down_kernel(lo_ref, hi_ref, nb_ref, g_hbm, w2_ref, *rest, n_lat):
    n = pl.program_id(0)
    e = pl.program_id(1)
    last_e = pl.num_programs(1) - 1
    resid = len(rest) == 11
    if resid:
        x_hbm, gt_ref, y_hbm, gbuf, obuf, zbuf, w2b, in_sem, out_sem, z_sem, xbuf = rest
    else:
        y_hbm, gbuf, obuf, zbuf, w2b, in_sem, out_sem, z_sem = rest
    tb, tn = obuf.shape[1:]
    w2b[...] = w2_ref[...].astype(BF16)

    def rows(b):
        return pl.ds(pl.multiple_of(b * tb, tb), tb)

    def cols(p):
        return pl.ds(pl.multiple_of(p * tn, tn), tn)

    def loads(b, p, slot):
        cps = [pltpu.make_async_copy(g_hbm.at[rows(b), :], gbuf.at[slot], in_sem.at[0, slot])]
        if resid:
            cps.append(pltpu.make_async_copy(x_hbm.at[rows(b), cols(p)], xbuf.at[slot], in_sem.at[1, slot]))
        return cps

    def stores(b, p, slot):
        return (pltpu.make_async_copy(obuf.at[slot], y_hbm.at[rows(b), cols(p)], out_sem.at[slot]),)

    def compute(b, slot):
        y = jnp.dot(gbuf[slot], w2b[...], preferred_element_type=F32)
        if resid:
            y = xbuf[slot] + _row_select(gt_ref, b * tb, tb, n_lat) * y
        obuf[slot] = y

    _sweep_blocks(lo_ref[e], hi_ref[e], hi_ref[last_e], n, pl.num_programs(0), loads, compute, stores)

    @pl.when(e == last_e)
    def _():
        _zero_blocks(hi_ref[e], nb_ref[0], zbuf,
                     lambda b: pltpu.make_async_copy(zbuf, y_hbm.at[rows(b), cols(n)], z_sem.at[0]))


def _swiglu_down(gact, blk_lo, blk_hi, w2, layer, tb, resid=None):
    rows, f = gact.shape
    n_exp, _, d = w2.shape[1:]
    wb = w2.dtype.itemsize
    tn = _fit_tile(d, (1024, 512, 256),
                   lambda t: 2 * f * t * wb + 2 * f * t + 4 * tb * f + 16 * tb * t + 4 * tb * t)
    nb = jnp.full((1,), rows // tb, I32)
    in_specs = [pl.BlockSpec(memory_space=pl.ANY),
                pl.BlockSpec((None, None, f, tn), lambda n, e, lo, hi, nb: (layer, e, 0, n))]
    args = [gact, w2]
    scratch = [pltpu.VMEM((2, tb, f), gact.dtype), pltpu.VMEM((2, tb, tn), F32), pltpu.VMEM((tb, tn), F32),
               pltpu.VMEM((f, tn), BF16),
               pltpu.SemaphoreType.DMA((2, 2)), pltpu.SemaphoreType.DMA((2,)), pltpu.SemaphoreType.DMA((1,))]
    n_lat = None
    if resid is not None:
        x, gt, n_lat = resid
        in_specs += [pl.BlockSpec(memory_space=pl.ANY),
                     pl.BlockSpec((2, tn), lambda n, e, lo, hi, nb: (0, n))]
        args += [x, gt]
        scratch.append(pltpu.VMEM((2, tb, tn), F32))
    return pl.pallas_call(
        functools.partial(_down_kernel, n_lat=n_lat),
        out_shape=jax.ShapeDtypeStruct((rows, d), F32),
        grid_spec=pltpu.PrefetchScalarGridSpec(
            num_scalar_prefetch=3,
            grid=(d // tn, n_exp),
            in_specs=in_specs,
            out_specs=pl.BlockSpec(memory_space=pl.ANY),
            scratch_shapes=scratch,
        ),
        compiler_params=_params(("arbitrary", "arbitrary")),
        name="swiglu_down",
    )(blk_lo, blk_hi, nb, *args)


def _dense_ffn(x, g2, sh, sc, gt, w1, w3, w2, layer, n_lat):
    r = x.shape[0]
    tb = _pick(r, (640, 256))
    blk_lo = jnp.zeros((1,), I32)
    blk_hi = jnp.full((1,), r // tb, I32)
    h = _ffn_norm(x, g2, sh, sc, n_lat)
    gact = _swiglu_up(h, blk_lo, blk_hi, w1[:, None], w3[:, None], layer, tb)
    return _swiglu_down(gact, blk_lo, blk_hi, w2[:, None], layer, tb, resid=(x, gt, n_lat))


def _route_kernel(x_ref, g_ref, sh_ref, sc_ref, wr_ref, br_ref, mi_ref, mf_ref, cnt_ref, h_s, run_s, gs_s, sh_s,
                  *, n_lat, n_exp):
    i = pl.program_id(0)
    tm = x_ref.shape[0]

    @pl.when(i == 0)
    def _():
        run_s[...] = jnp.zeros_like(run_s)

    _norm_mod_rows(x_ref, g_ref, sh_ref, sc_ref, h_s, gs_s, sh_s, i * tm, n_lat)
    h = h_s[...]
    w = wr_ref[...]
    h_hi = h.astype(BF16)
    h_lo = (h - h_hi.astype(F32)).astype(BF16)
    w_hi = w.astype(BF16)
    w_lo = (w - w_hi.astype(F32)).astype(BF16)
    logits = (jnp.dot(h_hi, w_hi, preferred_element_type=F32) + jnp.dot(h_hi, w_lo, preferred_element_type=F32)
              + jnp.dot(h_lo, w_hi, preferred_element_type=F32)) + br_ref[...]
    lane = lax.broadcasted_iota(I32, logits.shape, 1).astype(F32)
    l1 = jnp.where(lane < n_exp, logits, -jnp.inf)
    v1 = l1.max(axis=1, keepdims=True)
    e1 = jnp.where(l1 == v1, lane, float(V7X_LANES)).min(axis=1, keepdims=True)
    l2 = jnp.where(lane == e1, -jnp.inf, l1)
    v2 = l2.max(axis=1, keepdims=True)
    e2 = jnp.where(l2 == v2, lane, float(V7X_LANES)).min(axis=1, keepdims=True)
    t = jnp.exp(v2 - v1)
    g1 = 1.0 / (1.0 + t)
    g2 = t / (1.0 + t)

    onehot = jnp.where(jnp.logical_or(lane == e1, lane == e2), 1.0, 0.0)
    below = lax.broadcasted_iota(I32, (tm, tm), 0) > lax.broadcasted_iota(I32, (tm, tm), 1)
    before = jnp.dot(jnp.where(below, 1.0, 0.0).astype(BF16), onehot.astype(BF16),
                     preferred_element_type=F32) + run_s[...]
    r1 = jnp.where(lane == e1, before, 0.0).sum(axis=1, keepdims=True)
    r2 = jnp.where(lane == e2, before, 0.0).sum(axis=1, keepdims=True)
    run_s[...] = run_s[...] + onehot.sum(axis=0, keepdims=True)

    meta = jnp.where(lane == 0, e1, jnp.where(lane == 1, e2, jnp.where(lane == 2, r1, jnp.where(lane == 3, r2, 0.0))))
    mi_ref[...] = meta.astype(I32)
    mf_ref[...] = jnp.where(lane == 0, g1, jnp.where(lane == 1, g2, 0.0))
    cnt_ref[...] = jnp.broadcast_to(run_s[...], cnt_ref.shape)


def _route(x, g2, sh, sc, wr_pad, br_pad, n_lat, n_exp):
    r, d = x.shape
    tm = ROW_TILE
    return pl.pallas_call(
        functools.partial(_route_kernel, n_lat=n_lat, n_exp=n_exp),
        out_shape=(jax.ShapeDtypeStruct((r, V7X_LANES), I32),
                   jax.ShapeDtypeStruct((r, V7X_LANES), F32),
                   jax.ShapeDtypeStruct((V7X_SUBLANES, V7X_LANES), F32)),
        grid=(r // tm,),
        in_specs=[pl.BlockSpec((tm, d), lambda i: (i, 0)),
                  pl.BlockSpec((1, d), lambda i: (0, 0)),
                  pl.BlockSpec((2, d), lambda i: (0, 0)),
                  pl.BlockSpec((2, d), lambda i: (0, 0)),
                  pl.BlockSpec((d, V7X_LANES), lambda i: (0, 0)),
                  pl.BlockSpec((1, V7X_LANES), lambda i: (0, 0))],
        out_specs=(pl.BlockSpec((tm, V7X_LANES), lambda i: (i, 0)),
                   pl.BlockSpec((tm, V7X_LANES), lambda i: (i, 0)),
                   pl.BlockSpec((V7X_SUBLANES, V7X_LANES), lambda i: (0, 0))),
        scratch_shapes=[pltpu.VMEM((tm, d), F32), pltpu.VMEM((1, V7X_LANES), F32)] + _norm_scratch(d),
        compiler_params=_params(("arbitrary",)),
        name="route",
    )(x, g2, sh, sc, wr_pad, br_pad)


def _row_copy(src, s_row, dst, d_row, sem):
    return pltpu.make_async_copy(src.at[pl.ds(s_row, 1), :], dst.at[pl.ds(d_row, 1), :], sem)


def _dispatch_kernel(d1_ref, d2_ref, x_ref, g_ref, sh_ref, sc_ref, xb_in_ref, xb_ref, h_s, pk_s, sem, gs_s, sh_s,
                     *, n_lat):
    del xb_in_ref
    i = pl.program_id(0)
    tm, d = x_ref.shape
    _norm_mod_rows(x_ref, g_ref, sh_ref, sc_ref, h_s, gs_s, sh_s, i * tm, n_lat)
    pk_s[...] = _pack_bf16_pairs(h_s[:, :d // 2], h_s[:, d // 2:])

    def start(r, c):
        _row_copy(pk_s, r, xb_ref, d1_ref[0, r], sem.at[0]).start(priority=0)
        _row_copy(pk_s, r, xb_ref, d2_ref[0, r], sem.at[1]).start(priority=1)
        return c

    def wait(r, c):
        _row_copy(pk_s, r, xb_ref, d1_ref[0, r], sem.at[0]).wait()
        _row_copy(pk_s, r, xb_ref, d2_ref[0, r], sem.at[1]).wait()
        return c

    lax.fori_loop(0, tm, start, 0, unroll=ROW_DMA_UNROLL)
    lax.fori_loop(0, tm, wait, 0, unroll=ROW_DMA_UNROLL)


def _dispatch(x, g2, sh, sc, d1, d2, n_rows, n_lat):
    r, d = x.shape
    tm = ROW_TILE
    nt = r // tm
    smem_rows = pl.BlockSpec((None, 1, tm), lambda i: (i, 0, 0), memory_space=pltpu.SMEM)
    return pl.pallas_call(
        functools.partial(_dispatch_kernel, n_lat=n_lat),
        out_shape=jax.ShapeDtypeStruct((n_rows, d // 2), U32),
        grid=(nt,),
        in_specs=[smem_rows, smem_rows,
                  pl.BlockSpec((tm, d), lambda i: (i, 0)),
                  pl.BlockSpec((1, d), lambda i: (0, 0)),
                  pl.BlockSpec((2, d), lambda i: (0, 0)),
                  pl.BlockSpec((2, d), lambda i: (0, 0)),
                  pl.BlockSpec(memory_space=pl.ANY)],
        out_specs=pl.BlockSpec(memory_space=pl.ANY),
        scratch_shapes=[pltpu.VMEM((tm, d), F32), pltpu.VMEM((tm, d // 2), U32),
                        pltpu.SemaphoreType.DMA((2,))] + _norm_scratch(d),
        input_output_aliases={6: 0},
        compiler_params=_params(("arbitrary",)),
        name="moe_dispatch",
    )(d1.reshape(nt, 1, tm), d2.reshape(nt, 1, tm), x, g2, sh, sc, jnp.zeros((n_rows, d // 2), U32))


def _combine_kernel(d1_ref, d2_ref, d1n_ref, d2n_ref, mf_ref, x_ref, gt_ref, yb_ref, o_ref, buf, sem, *, n_lat):
    i = pl.program_id(0)
    tm = x_ref.shape[0]
    slot = i & 1

    def gather(a_ref, b_ref, s, fn):
        def body(r, c):
            fn(_row_copy(yb_ref, a_ref[0, r], buf.at[s, 0], r, sem.at[s, 0]), 0)
            fn(_row_copy(yb_ref, b_ref[0, r], buf.at[s, 1], r, sem.at[s, 1]), 1)
            return c

        lax.fori_loop(0, tm, body, 0, unroll=ROW_DMA_UNROLL)

    start = lambda cp, queue: cp.start(priority=queue)
    wait = lambda cp, queue: cp.wait()

    @pl.when(i == 0)
    def _():
        gather(d1_ref, d2_ref, 0, start)

    @pl.when(i + 1 < pl.num_programs(0))
    def _():
        gather(d1n_ref, d2n_ref, 1 - slot, start)

    gather(d1_ref, d2_ref, slot, wait)
    f = mf_ref[:, 0:1] * buf[slot, 0] + mf_ref[:, 1:2] * buf[slot, 1]
    o_ref[...] = x_ref[...] + _row_select(gt_ref, i * tm, tm, n_lat) * f


def _combine(x, gt, yb, d1, d2, mf, n_lat, n_out):
    r, d = x.shape
    tm = ROW_TILE
    nt = r // tm
    n_steps = n_out // tm
    smem_rows = pl.BlockSpec((None, 1, tm), lambda i: (i, 0, 0), memory_space=pltpu.SMEM)
    smem_next = pl.BlockSpec((None, 1, tm), lambda i: (jnp.minimum(i + 1, n_steps - 1), 0, 0),
                             memory_space=pltpu.SMEM)
    d1, d2 = d1.reshape(nt, 1, tm), d2.reshape(nt, 1, tm)
    return pl.pallas_call(
        functools.partial(_combine_kernel, n_lat=n_lat),
        out_shape=jax.ShapeDtypeStruct((n_out, d), F32),
        grid=(n_steps,),
        in_specs=[smem_rows, smem_rows, smem_next, smem_next,
                  pl.BlockSpec((tm, V7X_LANES), lambda i: (i, 0)),
                  pl.BlockSpec((tm, d), lambda i: (i, 0)),
                  pl.BlockSpec((2, d), lambda i: (0, 0)),
                  pl.BlockSpec(memory_space=pl.ANY)],
        out_specs=pl.BlockSpec((tm, d), lambda i: (i, 0)),
        scratch_shapes=[pltpu.VMEM((2, 2, tm, d), F32), pltpu.SemaphoreType.DMA((2, 2))],
        compiler_params=_params(("arbitrary",)),
        name="moe_combine",
    )(d1, d2, d1, d2, mf, x, gt, yb)


def _moe_ffn(x, g2, sh, sc, gt, wr, br, w1, w3, w2, layer, n_lat, n_out):
    r, d = x.shape
    n_exp = wr.shape[1]
    blk = MOE_BLOCK
    wr_pad = jnp.zeros((d, V7X_LANES), F32).at[:, :n_exp].set(wr)
    br_pad = jnp.zeros((1, V7X_LANES), F32).at[0, :n_exp].set(br)
    mi, mf, cnt = _route(x, g2, sh, sc, wr_pad, br_pad, n_lat, n_exp)

    counts = cnt[0, :n_exp].astype(I32)
    padded = (counts + blk - 1) // blk * blk
    pad_end = jnp.cumsum(padded)
    pad_start = pad_end - padded
    d1 = pad_start[mi[:, 0]] + mi[:, 2]
    d2 = pad_start[mi[:, 1]] + mi[:, 3]
    n_rows = -(-(TOP_K * r + n_exp * (blk - 1)) // blk) * blk
    blk_lo = (pad_start // blk).astype(I32)
    blk_hi = (pad_end // blk).astype(I32)

    xb = _dispatch(x, g2, sh, sc, d1, d2, n_rows, n_lat)
    gact = _swiglu_up(xb, blk_lo, blk_hi, w1, w3, layer, blk)
    yb = _swiglu_down(gact, blk_lo, blk_hi, w2, layer, blk)
    return _combine(x, gt, yb, d1, d2, mf, n_lat, n_out)


def _rope_tables(n_lat, n_ctx):
    n_rows = n_lat // GRID_W
    axis_dim = HEAD_DIM // 2
    inv_freq = ROPE_THETA ** (-jnp.arange(0, axis_dim, 2, dtype=F32) / axis_dim)

    def axis_tables(n):
        ang = jnp.arange(n, dtype=F32)[:, None] * inv_freq[None, :]
        return jnp.cos(ang), jnp.sin(ang)

    per_row = lambda t: jnp.repeat(t, GRID_W, axis=0)
    per_col = lambda t: jnp.tile(t, (n_rows, 1))
    cr, sr = map(per_row, axis_tables(n_rows))
    cc, sc = map(per_col, axis_tables(GRID_W))
    cos_t = jnp.concatenate([cr, cr, cc, cc], axis=1)
    sin_t = jnp.concatenate([-sr, sr, -sc, sc], axis=1)
    cos_t = jnp.concatenate([cos_t, jnp.ones((n_ctx, HEAD_DIM), F32)], axis=0)
    sin_t = jnp.concatenate([sin_t, jnp.zeros((n_ctx, HEAD_DIM), F32)], axis=0)
    return cos_t, sin_t


def kernel(x, c, ctx, c_ctx, w_ada, b_ada, g_norm1, w_in, conv_w, conv_b, conv_ln_g, conv_ln_b, conv_pw, conv_pw_b, sgu_ln_g, sgu_ln_b, sgu_w, sgu_b, swa_q_g, swa_k_g, swa_sink, glb_q_g, glb_k_g, g_branch, w_out, g_norm2, ffn_w1, ffn_w3, ffn_w2, router_w, router_b, exp_w1, exp_w3, exp_w2):
    batch, n_lat, d = x.shape
    n_ctx = ctx.shape[1]
    depth = w_ada.shape[0]
    gw = d // N_GROUPS
    assert batch == 1 and n_lat % n_ctx == 0 and n_ctx % ROW_TILE == 0 and n_lat % GRID_W == 0
    assert conv_w.shape[1] // 2 < CONV_HALO and w_in.shape[2] == 4 * 2 * gw

    xs = jnp.concatenate([x[0], ctx[0]], axis=0)
    mods = _ada(jnp.stack([c[0], c_ctx], axis=1), w_ada, b_ada)
    cos_t, sin_t = _rope_tables(n_lat, n_ctx)
    row2 = lambda v: v.reshape(1, -1)

    for l in range(depth):
        sh1, sc1, gt1, sh2, sc2, gt2 = (mods[l, :, k * d:(k + 1) * d] for k in range(6))

        p = _inproj(xs, row2(g_norm1[l]), sh1, sc1, w_in[l].astype(BF16), n_lat)
        y_conv = _conv_group(p, conv_w[l], row2(conv_b[l]), row2(conv_ln_g[l]), row2(conv_ln_b[l]),
                             conv_pw[l].astype(BF16), row2(conv_pw_b[l]), n_lat)
        y_sgu = _sgu_group(p, row2(sgu_ln_g[l]), row2(sgu_ln_b[l]), sgu_w[l].astype(BF16), sgu_b[l].T)
        qt_s, k_s, vt_s, _ = _prep(p, row2(swa_q_g[l]), row2(swa_k_g[l]), cos_t, sin_t, block=2)
        y_swa = _swa(qt_s, k_s, vt_s, swa_sink[l], n_lat)
        qt_g, k_g, vt_g, kn = _prep(p, row2(glb_q_g[l]), row2(glb_k_g[l]), cos_t, sin_t, block=3)
        kmax = jnp.sqrt(jnp.max(kn[:, 0, :N_KV_HEADS], axis=0)) * KEY_NORM_MARGIN
        y_glb = _glb(qt_g, k_g, vt_g, kmax, n_lat)
        xs = _outproj((y_conv, y_sgu, y_swa, y_glb), row2(g_branch[l]), w_out[l].astype(BF16), xs, gt1, n_lat)

        if l % 2 == 0:
            xs = _dense_ffn(xs, row2(g_norm2[l]), sh2, sc2, gt2, ffn_w1, ffn_w3, ffn_w2, l // 2, n_lat)
        else:
            xs = _moe_ffn(xs, row2(g_norm2[l]), sh2, sc2, gt2, router_w[l // 2], router_b[l // 2],
                          exp_w1, exp_w3, exp_w2, l // 2, n_lat, n_out=n_lat if l == depth - 1 else n_lat + n_ctx)
    return xs[:n_lat][None]
```
